```python
import jax, jax.numpy as jnp
from jax import lax
import numpy as np

D_MODEL = 1024
BATCH = 8
SEQ = 4096
DEPTH = 1

ATTN_HEADS = 8
HEAD_DIM = D_MODEL // 16
ATTN_WIDTH = ATTN_HEADS * HEAD_DIM
DILATED_BRANCHES = ((128, 1), (512, 4), (2048, 16))
ATTN_BLOCK = 128
SGU_GROUPS = 8
SGU_GROUP_DIM = D_MODEL // 16
SGU_WIDTH = SGU_GROUPS * SGU_GROUP_DIM
SGU_CHUNK = 128
MIX_WIDTH = ATTN_WIDTH + SGU_WIDTH
IN_WIDTH = 3 * ATTN_WIDTH + 2 * SGU_WIDTH
PEER_HEADS = 8
PEER_KEY_DIM = 256
N_SUB_KEYS = 128
N_EXPERTS = N_SUB_KEYS * N_SUB_KEYS
PEER_TOPK = 16
PEER_TOKEN_BLOCK = 128
EPS = 1e-6
NEG_INF = -1e30

kernel_name = 'hybrid_dilated_attn_sgu_peer_layer'


def rms_norm(x, g):
    xf = x.astype(jnp.float32)
    y = xf * lax.rsqrt(jnp.mean(xf * xf, axis=-1, keepdims=True) + EPS)
    return (y * g.astype(jnp.float32)).astype(x.dtype)


def alibi_slopes(n_heads):
    i = jnp.arange(1, n_heads + 1, dtype=jnp.float32)
    return jnp.exp2(-8.0 * i / n_heads)


def dilated_branch(q, k, v, slopes, window, dilation):
    b, s, h, dh = q.shape
    sub_len = s // dilation
    w_sub = window // dilation
    c = ATTN_BLOCK
    n = -(-sub_len // c)
    padded = n * c

    def to_sub(t):
        t = t.reshape(b, sub_len, dilation, h, dh).transpose(0, 2, 3, 1, 4)
        return jnp.pad(t, ((0, 0), (0, 0), (0, 0), (0, padded - sub_len), (0, 0)))

    def band(t):
        tp = jnp.pad(t, ((0, 0), (0, 0), (0, 0), (c, 0), (0, 0)))
        prev = tp[:, :, :, :padded].reshape(b, dilation, h, n, c, dh)
        cur = t.reshape(b, dilation, h, n, c, dh)
        return jnp.concatenate([prev, cur], axis=4)

    qs, ks, vs = to_sub(q), to_sub(k), to_sub(v)
    qb = qs.reshape(b, dilation, h, n, c, dh)
    kb, vb = band(ks), band(vs)
    scores = jnp.einsum('bdhnqe,bdhnke->bdhnqk', qb, kb).astype(jnp.float32) * (dh ** -0.5)
    qi = jnp.arange(c)[:, None]
    ki = jnp.arange(2 * c)[None, :]
    dist = c + qi - ki
    blk = jnp.arange(n)[:, None, None]
    valid = (dist >= 0) & (dist <= w_sub) & ((blk > 0) | (ki >= c))
    bias = -slopes[:, None, None, None] * (dilation * dist).astype(jnp.float32)
    scores = jnp.where(valid, scores + bias, NEG_INF)
    m = jnp.max(scores, axis=-1)
    p = jnp.exp(scores - m[..., None])
    l = jnp.sum(p, axis=-1)
    o = jnp.einsum('bdhnqk,bdhnke->bdhnqe', p, vb.astype(jnp.float32)) / l[..., None]

    def from_sub(t):
        t = t.reshape((b, dilation, h, padded) + t.shape[5:])[:, :, :, :sub_len]
        perm = (0, 3, 1, 2) + tuple(range(4, t.ndim))
        return t.transpose(perm).reshape((b, s, h) + t.shape[4:])

    return from_sub(o), from_sub(m), from_sub(l)


def dilated_attention(q, k, v, slopes):
    outs = [dilated_branch(q, k, v, slopes, w, d) for (w, d) in DILATED_BRANCHES]
    m_all = jnp.stack([m for (_, m, _) in outs])
    m_max = jnp.max(m_all, axis=0)
    wts = jnp.stack([l * jnp.exp(m - m_max) for (_, m, l) in outs])
    num = sum(wts[i][..., None] * outs[i][0] for i in range(len(outs)))
    return (num / jnp.sum(wts, axis=0)[..., None]).astype(q.dtype)


def spatial_gating(su, sv, sgu_norm_g, sgu_w, sgu_b):
    b, s, _ = su.shape
    nc = s // SGU_CHUNK
    svn = rms_norm(sv.reshape(b, s, SGU_GROUPS, SGU_GROUP_DIM), sgu_norm_g.reshape(SGU_GROUPS, SGU_GROUP_DIM))
    svn = svn.reshape(b, nc, SGU_CHUNK, SGU_GROUPS, SGU_GROUP_DIM)
    causal = jnp.tril(jnp.ones((SGU_CHUNK, SGU_CHUNK), dtype=bool))
    w = jnp.where(causal[None], sgu_w, 0.0).astype(sgu_w.dtype)
    mixed = jnp.einsum('gts,bcsge->bctge', w, svn) + sgu_b.T[:, :, None]
    return su * mixed.reshape(b, s, SGU_WIDTH)


def peer(h, w_query, sub_keys, expert_u, expert_v):
    b, s, d = h.shape
    t = b * s
    hf = h.reshape(t, d)
    q = (hf @ w_query).reshape(t, PEER_HEADS, 2, PEER_KEY_DIM // 2).astype(jnp.float32)
    s1 = jnp.einsum('thk,nk->thn', q[:, :, 0], sub_keys[0].astype(jnp.float32))
    s2 = jnp.einsum('thk,nk->thn', q[:, :, 1], sub_keys[1].astype(jnp.float32))
    v1, i1 = lax.top_k(s1, PEER_TOPK)
    v2, i2 = lax.top_k(s2, PEER_TOPK)
    cand = (v1[..., :, None] + v2[..., None, :]).reshape(t, PEER_HEADS, PEER_TOPK * PEER_TOPK)
    cidx = (i1[..., :, None] * N_SUB_KEYS + i2[..., None, :]).reshape(t, PEER_HEADS, PEER_TOPK * PEER_TOPK)
    best, pos = lax.top_k(cand, PEER_TOPK)
    eidx = jnp.take_along_axis(cidx, pos, axis=-1).reshape(t, PEER_HEADS * PEER_TOPK)
    gates = jax.nn.softmax(best, axis=-1).reshape(t, PEER_HEADS * PEER_TOPK)
    nb = t // PEER_TOKEN_BLOCK

    def block(args):
        hb, eb, gb = args
        u = jnp.take(expert_u, eb, axis=0)
        a = jnp.einsum('cd,ced->ce', hb, u).astype(jnp.float32)
        coef = gb * jax.nn.gelu(a, approximate=False)
        vv = jnp.take(expert_v, eb, axis=0)
        return jnp.einsum('ce,ced->cd', coef.astype(vv.dtype), vv)

    out = lax.map(block, (hf.reshape(nb, PEER_TOKEN_BLOCK, d),
                          eidx.reshape(nb, PEER_TOKEN_BLOCK, PEER_HEADS * PEER_TOPK),
                          gates.reshape(nb, PEER_TOKEN_BLOCK, PEER_HEADS * PEER_TOPK)))
    return out.reshape(b, s, d).astype(h.dtype)


def setup_inputs(seed: int = 0) -> dict:
    key = jax.random.key(seed)
    ks = jax.random.split(key, 16)
    f32 = jnp.float32
    nrm = lambda k, shp: jax.random.normal(k, shp, f32)
    gain = lambda k, shp: 1.0 + 0.01 * nrm(k, shp)
    return {
        'x': nrm(ks[0], (BATCH, SEQ, D_MODEL)),
        'norm1_g': gain(ks[1], (DEPTH, D_MODEL)),
        'w_in': nrm(ks[2], (DEPTH, D_MODEL, IN_WIDTH)) * D_MODEL ** -0.5,
        'q_norm_g': gain(ks[3], (DEPTH, HEAD_DIM)),
        'k_norm_g': gain(ks[4], (DEPTH, HEAD_DIM)),
        'sgu_norm_g': gain(ks[5], (DEPTH, SGU_WIDTH)),
        'sgu_w': nrm(ks[6], (DEPTH, SGU_GROUPS, SGU_CHUNK, SGU_CHUNK)) * SGU_CHUNK ** -0.5,
        'sgu_b': 1.0 + 0.1 * nrm(ks[7], (DEPTH, SGU_GROUPS, SGU_CHUNK)),
        'attn_out_g': gain(ks[8], (DEPTH, ATTN_WIDTH)),
        'sgu_out_g': gain(ks[9], (DEPTH, SGU_WIDTH)),
        'w_out': nrm(ks[10], (DEPTH, MIX_WIDTH, D_MODEL)) * MIX_WIDTH ** -0.5,
        'norm2_g': gain(ks[11], (DEPTH, D_MODEL)),
        'w_query': nrm(ks[12], (DEPTH, D_MODEL, PEER_HEADS * PEER_KEY_DIM)) * D_MODEL ** -0.5,
        'sub_keys': nrm(ks[13], (DEPTH, 2, N_SUB_KEYS, PEER_KEY_DIM // 2)) * (PEER_KEY_DIM // 2) ** -0.5,
        'expert_u': nrm(ks[14], (DEPTH, N_EXPERTS, D_MODEL)) * D_MODEL ** -0.5,
        'expert_v': nrm(ks[15], (DEPTH, N_EXPERTS, D_MODEL)) * 0.1,
    }


def reference(x, norm1_g, w_in, q_norm_g, k_norm_g, sgu_norm_g, sgu_w, sgu_b, attn_out_g, sgu_out_g,
              w_out, norm2_g, w_query, sub_keys, expert_u, expert_v):
    b, s, _ = x.shape
    slopes = alibi_slopes(ATTN_HEADS)
    splits = [ATTN_WIDTH, 2 * ATTN_WIDTH, 3 * ATTN_WIDTH, 3 * ATTN_WIDTH + SGU_WIDTH]
    for layer in range(DEPTH):
        h = rms_norm(x, norm1_g[layer])
        proj = h @ w_in[layer]
        q, k, v, su, sv = jnp.split(proj, splits, axis=-1)
        q = rms_norm(q.reshape(b, s, ATTN_HEADS, HEAD_DIM), q_norm_g[layer])
        k = rms_norm(k.reshape(b, s, ATTN_HEADS, HEAD_DIM), k_norm_g[layer])
        v = v.reshape(b, s, ATTN_HEADS, HEAD_DIM)
        attn = dilated_attention(q, k, v, slopes).reshape(b, s, ATTN_WIDTH)
        gated = spatial_gating(su, sv, sgu_norm_g[layer], sgu_w[layer], sgu_b[layer])
        mixed = jnp.concatenate([rms_norm(attn, attn_out_g[layer]),
                                 rms_norm(gated, sgu_out_g[layer])], axis=-1)
        x = x + mixed @ w_out[layer]
        h2 = rms_norm(x, norm2_g[layer])
        x = x + peer(h2, w_query[layer], sub_keys[layer], expert_u[layer], expert_v[layer])
    return x
```

```python
import functools

import jax
import jax.numpy as jnp
from jax import lax
from jax.experimental import pallas as pl
from jax.experimental.pallas import tpu as pltpu

F32 = jnp.float32
BF16 = jnp.bfloat16

D_MODEL = 1024
BATCH = 8
SEQ = 4096
TOKENS = BATCH * SEQ
ATTN_HEADS = 8
HEAD_DIM = 64
ATTN_WIDTH = 512
SGU_GROUPS = 8
SGU_WIDTH = 512
SGU_CHUNK = 128
IN_WIDTH = 3 * ATTN_WIDTH + 2 * SGU_WIDTH
DILATED_BRANCHES = ((128, 1), (512, 4), (2048, 16))
ATTN_BLOCK = 128
PEER_HEADS = 8
N_SUB_KEYS = 128
N_EXPERTS = N_SUB_KEYS * N_SUB_KEYS
PEER_TOPK = 16
HK = PEER_HEADS * PEER_TOPK
EPS = 1e-6
NEG_INF = -1e30

LANES = 128
TM = 512
TB = 128
ROWS_PER_EXPERT = 4
VMEM_LIMIT = 56 * 1024 * 1024


def _rms(x, gain):
    return x * lax.rsqrt(jnp.mean(x * x, axis=-1, keepdims=True) + EPS) * gain


def _group_rms(t, bd):
    sq = t * t
    hi = sq.astype(BF16)
    lo = (sq - hi.astype(F32)).astype(BF16)
    ssq = jnp.dot(hi, bd, preferred_element_type=F32) + jnp.dot(lo, bd, preferred_element_type=F32)
    return t * lax.rsqrt(ssq * (1.0 / HEAD_DIM) + EPS)


def _in_proj_kernel(x_ref, g1_ref, win_ref, qg_ref, kg_ref, sg_ref, bd_ref, wsgu_ref, bsgu_ref, sog_ref,
                    qn_ref, kn_ref, v_ref, gn_ref):
    h = _rms(x_ref[...], g1_ref[...])
    proj = jnp.dot(h.astype(BF16), win_ref[...], preferred_element_type=F32)
    bd = bd_ref[...]
    for c in range(ATTN_WIDTH // LANES):
        cs = slice(c * LANES, (c + 1) * LANES)
        qn_ref[:, cs] = _group_rms(proj[:, c * LANES:(c + 1) * LANES], bd) * qg_ref[:, cs]
        kn_ref[:, cs] = _group_rms(proj[:, ATTN_WIDTH + c * LANES:ATTN_WIDTH + (c + 1) * LANES], bd) * kg_ref[:, cs]
    v_ref[...] = proj[:, 2 * ATTN_WIDTH:3 * ATTN_WIDTH]

    su0 = 3 * ATTN_WIDTH
    sv0 = su0 + SGU_WIDTH
    row = lax.broadcasted_iota(jnp.int32, (SGU_CHUNK, SGU_CHUNK), 0)
    col = lax.broadcasted_iota(jnp.int32, (SGU_CHUNK, SGU_CHUNK), 1)
    causal = col <= row
    left = lax.broadcasted_iota(jnp.int32, (SGU_CHUNK, LANES), 1) < HEAD_DIM
    gated_cols = []
    for p in range(SGU_WIDTH // LANES):
        cs = slice(p * LANES, (p + 1) * LANES)
        svn = (_group_rms(proj[:, sv0 + p * LANES:sv0 + (p + 1) * LANES], bd) * sg_ref[:, cs]).astype(BF16)
        w0 = jnp.where(causal, wsgu_ref[2 * p], 0.0).astype(BF16)
        w1 = jnp.where(causal, wsgu_ref[2 * p + 1], 0.0).astype(BF16)
        chunks = []
        for c in range(TM // SGU_CHUNK):
            blk = svn[c * SGU_CHUNK:(c + 1) * SGU_CHUNK, :]
            r0 = jnp.dot(w0, blk, preferred_element_type=F32)
            r1 = jnp.dot(w1, blk, preferred_element_type=F32)
            chunks.append(jnp.where(left, r0, r1) + bsgu_ref[:, cs])
        mixed = jnp.concatenate(chunks, axis=0)
        gated_cols.append(proj[:, su0 + p * LANES:su0 + (p + 1) * LANES] * mixed)
    gated = jnp.concatenate(gated_cols, axis=1)
    gn_ref[...] = _rms(gated, sog_ref[...])


def _in_proj(x, g1, win, qg, kg, sg, bd, wsgu, bsgu, sog):
    full = lambda *shape: pl.BlockSpec(shape, lambda i: (0,) * len(shape))
    tile = lambda width: pl.BlockSpec((TM, width), lambda i: (i, 0))
    out = jax.ShapeDtypeStruct((TOKENS, ATTN_WIDTH), F32)
    return pl.pallas_call(
        _in_proj_kernel,
        grid=(TOKENS // TM,),
        in_specs=[tile(D_MODEL), full(1, D_MODEL), full(D_MODEL, IN_WIDTH), full(1, ATTN_WIDTH), full(1, ATTN_WIDTH),
                  full(1, SGU_WIDTH), full(LANES, LANES), full(SGU_GROUPS, SGU_CHUNK, SGU_CHUNK),
                  full(SGU_CHUNK, SGU_WIDTH), full(1, SGU_WIDTH)],
        out_specs=[tile(ATTN_WIDTH)] * 4,
        out_shape=[out] * 4,
        compiler_params=pltpu.CompilerParams(dimension_semantics=("parallel",), vmem_limit_bytes=VMEM_LIMIT),
        name="in_proj",
    )(x, g1, win, qg, kg, sg, bd, wsgu, bsgu, sog)


def _attn_kernel(slopes_ref, q_ref, k_ref, v_ref, o_ref, *scr):
    hp = pl.program_id(1)
    left = lax.broadcasted_iota(jnp.int32, (ATTN_BLOCK, LANES), 1) < HEAD_DIM
    qi = lax.broadcasted_iota(jnp.int32, (ATTN_BLOCK, 2 * ATTN_BLOCK), 0)
    ki = lax.broadcasted_iota(jnp.int32, (ATTN_BLOCK, 2 * ATTN_BLOCK), 1)
    base = ATTN_BLOCK + qi - ki

    for br, (window, d) in enumerate(DILATED_BRANCHES):
        nblk = SEQ // d // ATTN_BLOCK
        w_sub = window // d
        o_scr, m_scr, l_scr = scr[3 * br:3 * br + 3]

        def rows(start, size):
            return pl.ds(start, size) if d == 1 else pl.ds(start, size, stride=d)

        def body(idx, carry):
            r = idx // nblk
            n = idx % nblk
            first = jnp.maximum(n - 1, 0)
            qs = d * ATTN_BLOCK * n + r
            ks = d * ATTN_BLOCK * first + r
            qb = q_ref[0, rows(qs, ATTN_BLOCK), :]
            kb = k_ref[0, rows(ks, 2 * ATTN_BLOCK), :].astype(BF16)
            vb = v_ref[0, rows(ks, 2 * ATTN_BLOCK), :].astype(BF16)
            dist = base - ATTN_BLOCK * (1 - (n - first))
            valid = (dist >= 0) & (dist <= w_sub)
            distf = (d * dist).astype(F32)
            res = []
            for hh in range(2):
                slope = slopes_ref[2 * hp + hh]
                qh = jnp.where(left if hh == 0 else jnp.logical_not(left), qb, 0.0).astype(BF16)
                s = lax.dot_general(qh, kb, (((1,), (1,)), ((), ())), preferred_element_type=F32)
                s = jnp.where(valid, s * (HEAD_DIM ** -0.5) - slope * distf, NEG_INF)
                m = jnp.max(s, axis=-1, keepdims=True)
                p = jnp.exp(s - m)
                l = jnp.sum(p, axis=-1, keepdims=True)
                o = jnp.dot(p.astype(BF16), vb, preferred_element_type=F32)
                res.append((o, m, l))
            dst = rows(qs, ATTN_BLOCK)
            o_scr[dst, :] = jnp.where(left, res[0][0], res[1][0])
            m_scr[dst, :] = jnp.where(left, res[0][1], res[1][1])
            l_scr[dst, :] = jnp.where(left, res[0][2], res[1][2])
            return carry

        lax.fori_loop(0, SEQ // ATTN_BLOCK, body, 0)

    (o1, m1, l1), (o2, m2, l2), (o3, m3, l3) = [tuple(s[...] for s in scr[3 * b:3 * b + 3]) for b in range(3)]
    m_max = jnp.maximum(jnp.maximum(m1, m2), m3)
    w1 = jnp.exp(m1 - m_max)
    w2 = jnp.exp(m2 - m_max)
    w3 = jnp.exp(m3 - m_max)
    num = w1 * o1 + w2 * o2 + w3 * o3
    den = w1 * l1 + w2 * l2 + w3 * l3
    o_ref[0] = num / den


def _attention(slopes, qn, kn, v):
    blk = pl.BlockSpec((1, SEQ, LANES), lambda b, hp: (b, 0, hp))
    shape3 = (BATCH, SEQ, ATTN_WIDTH)
    return pl.pallas_call(
        _attn_kernel,
        grid=(BATCH, ATTN_WIDTH // LANES),
        in_specs=[pl.BlockSpec(memory_space=pltpu.SMEM), blk, blk, blk],
        out_specs=blk,
        out_shape=jax.ShapeDtypeStruct(shape3, F32),
        scratch_shapes=[pltpu.VMEM((SEQ, LANES), F32)] * 9,
        compiler_params=pltpu.CompilerParams(dimension_semantics=("parallel", "parallel"),
                                             vmem_limit_bytes=VMEM_LIMIT),
        name="attention",
    )(slopes, qn.reshape(shape3), kn.reshape(shape3), v.reshape(shape3)).reshape(TOKENS, ATTN_WIDTH)


def _out_proj_kernel(attn_ref, gn_ref, x_ref, ag_ref, wout_ref, g2_ref, wq_ref, x1_ref, h2_ref, qp_ref):
    an = _rms(attn_ref[...], ag_ref[...])
    mixed = (jnp.dot(an.astype(BF16), wout_ref[:ATTN_WIDTH, :], preferred_element_type=F32)
             + jnp.dot(gn_ref[...].astype(BF16), wout_ref[ATTN_WIDTH:, :], preferred_element_type=F32))
    x1 = x_ref[...] + mixed
    x1_ref[...] = x1
    h2 = _rms(x1, g2_ref[...])
    h2_ref[...] = h2
    qp_ref[...] = jnp.dot(h2.astype(BF16), wq_ref[...], preferred_element_type=F32)


def _out_proj(attn, gn, x, ag, wout, g2, wq):
    full = lambda *shape: pl.BlockSpec(shape, lambda i: (0,) * len(shape))
    tile = lambda width: pl.BlockSpec((TM, width), lambda i: (i, 0))
    qw = wq.shape[1]
    return pl.pallas_call(
        _out_proj_kernel,
        grid=(TOKENS // TM,),
        in_specs=[tile(ATTN_WIDTH), tile(SGU_WIDTH), tile(D_MODEL), full(1, ATTN_WIDTH), full(D_MODEL, D_MODEL),
                  full(1, D_MODEL), full(D_MODEL, qw)],
        out_specs=[tile(D_MODEL), tile(D_MODEL), tile(qw)],
        out_shape=[jax.ShapeDtypeStruct((TOKENS, D_MODEL), F32), jax.ShapeDtypeStruct((TOKENS, D_MODEL), F32),
                   jax.ShapeDtypeStruct((TOKENS, qw), F32)],
        compiler_params=pltpu.CompilerParams(dimension_semantics=("parallel",), vmem_limit_bytes=VMEM_LIMIT),
        name="out_proj",
    )(attn, gn, x, ag, wout, g2, wq)


def _top16(vals, ids, n_rows):
    pos = lax.broadcasted_iota(jnp.int32, (n_rows, TB), 0)
    best, picked = [], []
    for _ in range(PEER_TOPK):
        m = jnp.max(vals, axis=0, keepdims=True)
        p = jnp.min(jnp.where(vals == m, pos, n_rows), axis=0, keepdims=True)
        sel = pos == p
        best.append(m)
        picked.append(p if ids is None else jnp.max(jnp.where(sel, ids, -1), axis=0, keepdims=True))
        vals = jnp.where(sel, -jnp.inf, vals)
    return jnp.concatenate(best, axis=0), jnp.concatenate(picked, axis=0)


def _topk_kernel(qp_ref, sk_ref, eidx_ref, gates_ref):
    for h in range(PEER_HEADS):
        tops = []
        for half in range(2):
            c0 = (2 * h + half) * N_SUB_KEYS
            qh = qp_ref[:, c0:c0 + N_SUB_KEYS].astype(BF16)
            s = lax.dot_general(sk_ref[half], qh, (((1,), (1,)), ((), ())), preferred_element_type=F32)
            tops.append(_top16(s, None, N_SUB_KEYS))
        (v1, i1), (v2, i2) = tops
        cand = (v1[:, None, :] + v2[None, :, :]).reshape(PEER_TOPK * PEER_TOPK, TB)
        cidx = (i1[:, None, :] * N_SUB_KEYS + i2[None, :, :]).reshape(PEER_TOPK * PEER_TOPK, TB)
        best, eid = _top16(cand, cidx, PEER_TOPK * PEER_TOPK)
        ex = jnp.exp(best - best[0:1, :])
        rs = slice(h * PEER_TOPK, (h + 1) * PEER_TOPK)
        gates_ref[0, rs, :] = ex / jnp.sum(ex, axis=0, keepdims=True)
        eidx_ref[0, rs, :] = eid * ROWS_PER_EXPERT


def _peer_topk(qp, sk):
    nb = TOKENS // TB
    qw = qp.shape[1]
    out_blk = pl.BlockSpec((1, HK, TB), lambda i: (i, 0, 0))
    return pl.pallas_call(
        _topk_kernel,
        grid=(nb,),
        in_specs=[pl.BlockSpec((TB, qw), lambda i: (i, 0)),
                  pl.BlockSpec((2, N_SUB_KEYS, N_SUB_KEYS), lambda i: (0, 0, 0))],
        out_specs=[out_blk, out_blk],
        out_shape=[jax.ShapeDtypeStruct((nb, HK, TB), jnp.int32), jax.ShapeDtypeStruct((nb, HK, TB), F32)],
        compiler_params=pltpu.CompilerParams(dimension_semantics=("parallel",), vmem_limit_bytes=VMEM_LIMIT),
        name="peer_topk",
    )(qp, sk)


def _pack_table(w):
    wb = lax.bitcast_convert_type(w.astype(BF16), jnp.uint16).astype(jnp.uint32)
    half = D_MODEL // 2
    packed = (wb[:, half:] << 16) | wb[:, :half]
    return lax.bitcast_convert_type(packed, jnp.int32).reshape(N_EXPERTS * ROWS_PER_EXPERT, LANES)


def _unpack(w):
    return pltpu.bitcast(w << 16, F32), pltpu.bitcast(w & jnp.int32(-65536), F32)


def _gather_rows(tab, idx_smem, stage, off):
    for e in range(HK):
        row = idx_smem.at[pl.ds(e * TB, (HK + 1) * TB)][off]
        stage[ROWS_PER_EXPERT * e:ROWS_PER_EXPERT * (e + 1), :] = tab[pl.ds(pl.multiple_of(row, ROWS_PER_EXPERT),
                                                                              ROWS_PER_EXPERT), :]


def _peer_u_kernel(eidx_hbm, utab_hbm, h_ref, gates_ref, coef_ref, tab, idx_smem, stage, accs, tab_sem, idx_sem):
    i = pl.program_id(0)
    n = pl.num_programs(0)
    slot = lax.rem(i, 2)

    def idx_copy(blk, s):
        return pltpu.make_async_copy(eidx_hbm.at[blk], idx_smem.at[pl.ds(s * (HK * TB), HK * TB)], idx_sem.at[s])

    @pl.when(i == 0)
    def _():
        idx_copy(0, 0).start()
        cp = pltpu.make_async_copy(utab_hbm, tab, tab_sem)
        cp.start()
        cp.wait()

    idx_copy(i, slot).wait()

    @pl.when(i + 1 < n)
    def _():
        idx_copy(i + 1, 1 - slot).start()

    base = slot * (HK * TB)

    def tok(t, carry):
        _gather_rows(tab, idx_smem, stage, base + t)
        h = h_ref[t]
        acc = jnp.zeros((HK, LANES), F32)
        for s in range(ROWS_PER_EXPERT):
            lo, hi = _unpack(stage[pl.ds(s, HK, stride=ROWS_PER_EXPERT), :])
            acc = acc + lo * h[s:s + 1, :] + hi * h[ROWS_PER_EXPERT + s:ROWS_PER_EXPERT + s + 1, :]
        accs[t] = acc
        return carry

    lax.fori_loop(0, TB, tok, 0)

    lane = lax.broadcasted_iota(jnp.int32, (HK, TB), 1)
    group = 8

    def reduce(g, a):
        for j in range(group):
            t = g * group + j
            a = jnp.where(lane == t, jnp.sum(accs[t], axis=1, keepdims=True), a)
        return a

    a = lax.fori_loop(0, TB // group, reduce, jnp.zeros((HK, TB), F32))
    gelu = 0.5 * a * (1.0 + lax.erf(a * (2.0 ** -0.5)))
    coef_ref[0] = gates_ref[0] * gelu


def _peer_v_kernel(eidx_hbm, vtab_hbm, coef_ref, x1_ref, o_ref, tab, idx_smem, stage, tab_sem, idx_sem):
    i = pl.program_id(0)
    n = pl.num_programs(0)
    slot = lax.rem(i, 2)

    def idx_copy(blk, s):
        return pltpu.make_async_copy(eidx_hbm.at[blk], idx_smem.at[pl.ds(s * (HK * TB), HK * TB)], idx_sem.at[s])

    @pl.when(i == 0)
    def _():
        idx_copy(0, 0).start()
        cp = pltpu.make_async_copy(vtab_hbm, tab, tab_sem)
        cp.start()
        cp.wait()

    idx_copy(i, slot).wait()

    @pl.when(i + 1 < n)
    def _():
        idx_copy(i + 1, 1 - slot).start()

    base = slot * (HK * TB)
    lane = lax.broadcasted_iota(jnp.int32, (HK, TB), 1)

    def tok(t, carry):
        cb = jnp.sum(jnp.where(lane == t, coef_ref[0], 0.0), axis=1, keepdims=True)
        _gather_rows(tab, idx_smem, stage, base + t)
        lo_rows, hi_rows = [], []
        for s in range(ROWS_PER_EXPERT):
            lo, hi = _unpack(stage[pl.ds(s, HK, stride=ROWS_PER_EXPERT), :])
            lo_rows.append(jnp.sum(lo * cb, axis=0, keepdims=True))
            hi_rows.append(jnp.sum(hi * cb, axis=0, keepdims=True))
        o_ref[t] = x1_ref[t] + jnp.concatenate(lo_rows + hi_rows, axis=0)
        return carry

    lax.fori_loop(0, TB, tok, 0)


def _peer_gather(eidx, gates, h2, x1, utab, vtab):
    nb = TOKENS // TB
    eidx_flat = eidx.reshape(nb, HK * TB)
    any_spec = pl.BlockSpec(memory_space=pl.ANY)
    tok_blk = pl.BlockSpec((TB, 8, LANES), lambda i: (i, 0, 0))
    col_blk = pl.BlockSpec((1, HK, TB), lambda i: (i, 0, 0))
    table = pltpu.VMEM((N_EXPERTS * ROWS_PER_EXPERT, LANES), jnp.int32)
    idx_scr = pltpu.SMEM((2 * HK * TB,), jnp.int32)
    stage = pltpu.VMEM((HK * ROWS_PER_EXPERT, LANES), jnp.int32)
    params = pltpu.CompilerParams(dimension_semantics=("arbitrary",), vmem_limit_bytes=VMEM_LIMIT)
    coef = pl.pallas_call(
        _peer_u_kernel,
        grid=(nb,),
        in_specs=[any_spec, any_spec, tok_blk, col_blk],
        out_specs=col_blk,
        out_shape=jax.ShapeDtypeStruct((nb, HK, TB), F32),
        scratch_shapes=[table, idx_scr, stage, pltpu.VMEM((TB, HK, LANES), F32),
                        pltpu.SemaphoreType.DMA, pltpu.SemaphoreType.DMA((2,))],
        compiler_params=params,
        name="peer_u",
    )(eidx_flat, utab, h2.reshape(TOKENS, 8, LANES), gates)
    return pl.pallas_call(
        _peer_v_kernel,
        grid=(nb,),
        in_specs=[any_spec, any_spec, col_blk, tok_blk],
        out_specs=tok_blk,
        out_shape=jax.ShapeDtypeStruct((TOKENS, 8, LANES), F32),
        scratch_shapes=[table, idx_scr, stage, pltpu.SemaphoreType.DMA, pltpu.SemaphoreType.DMA((2,))],
        compiler_params=params,
        name="peer_v",
    )(eidx_flat, vtab, coef, x1.reshape(TOKENS, 8, LANES))


def kernel(x, norm1_g, w_in, q_norm_g, k_norm_g, sgu_norm_g, sgu_w, sgu_b, attn_out_g, sgu_out_g, w_out, norm2_g,
           w_query, sub_keys, expert_u, expert_v):
    b, s, _ = x.shape
    xf = x.reshape(TOKENS, D_MODEL)
    i = jnp.arange(1, ATTN_HEADS + 1, dtype=F32)
    slopes = jnp.exp2(-8.0 * i / ATTN_HEADS)
    g = lax.broadcasted_iota(jnp.int32, (LANES, LANES), 0) // HEAD_DIM
    bd = (g == g.T).astype(BF16)
    for layer in range(norm1_g.shape[0]):
        qn, kn, v, gn = _in_proj(
            xf, norm1_g[layer][None], w_in[layer].astype(BF16),
            jnp.tile(q_norm_g[layer], ATTN_HEADS)[None], jnp.tile(k_norm_g[layer], ATTN_HEADS)[None],
            sgu_norm_g[layer][None], bd, sgu_w[layer],
            jnp.repeat(sgu_b[layer].T, HEAD_DIM, axis=1), sgu_out_g[layer][None])
        attn = _attention(slopes, qn, kn, v)
        x1, h2, qp = _out_proj(attn, gn, xf, attn_out_g[layer][None], w_out[layer].astype(BF16),
                               norm2_g[layer][None], w_query[layer].astype(BF16))
        eidx, gates = _peer_topk(qp, sub_keys[layer].astype(BF16))
        xf = _peer_gather(eidx, gates, h2, x1, _pack_table(expert_u[layer]),
                          _pack_table(expert_v[layer])).reshape(TOKENS, D_MODEL)
    return xf.reshape(b, s, D_MODEL)
```

```python
import functools

import jax
import jax.numpy as jnp
from jax import lax
from jax.experimental import pallas as pl
from jax.experimental.pallas import tpu as pltpu

F32 = jnp.float32
BF16 = jnp.bfloat16

D_MODEL = 1024
BATCH = 8
SEQ = 4096
TOKENS = BATCH * SEQ
ATTN_HEADS = 8
HEAD_DIM = 64
ATTN_WIDTH = 512
SGU_GROUPS = 8
SGU_WIDTH = 512
SGU_CHUNK = 128
IN_WIDTH = 3 * ATTN_WIDTH + 2 * SGU_WIDTH
DILATED_BRANCHES = ((128, 1), (512, 4), (2048, 16))
ATTN_BLOCK = 128
ATTN_UNROLL = 4
PEER_HEADS = 8
N_SUB_KEYS = 128
N_EXPERTS = N_SUB_KEYS * N_SUB_KEYS
PEER_TOPK = 16
TOPK_UNROLL = 2
HK = PEER_HEADS * PEER_TOPK
N_STAGE = 2
GATHER_AHEAD = 1
EPS = 1e-6
NEG_INF = -1e30

LANES = 128
TM = 512
TB = 128
ROWS_PER_EXPERT = 4
TABLE_PAD = 4
VMEM_LIMIT = 56 * 1024 * 1024


def _rms(x, gain):
    return x * lax.rsqrt(jnp.mean(x * x, axis=-1, keepdims=True) + EPS) * gain


def _group_rms(t, bd):
    sq = t * t
    hi = sq.astype(BF16)
    lo = (sq - hi.astype(F32)).astype(BF16)
    ssq = jnp.dot(hi, bd, preferred_element_type=F32) + jnp.dot(lo, bd, preferred_element_type=F32)
    return t * lax.rsqrt(ssq * (1.0 / HEAD_DIM) + EPS)


def _in_proj_kernel(x_ref, g1_ref, win_ref, qg_ref, kg_ref, sg_ref, bd_ref, wsgu_ref, bsgu_ref, sog_ref,
                    qn_ref, kn_ref, v_ref, gn_ref):
    h = _rms(x_ref[...], g1_ref[...])
    proj = jnp.dot(h.astype(BF16), win_ref[...], preferred_element_type=F32)
    bd = bd_ref[...]
    for c in range(ATTN_WIDTH // LANES):
        cs = slice(c * LANES, (c + 1) * LANES)
        qn_ref[:, cs] = _group_rms(proj[:, c * LANES:(c + 1) * LANES], bd) * qg_ref[:, cs]
        kn_ref[:, cs] = _group_rms(proj[:, ATTN_WIDTH + c * LANES:ATTN_WIDTH + (c + 1) * LANES], bd) * kg_ref[:, cs]
    v_ref[...] = proj[:, 2 * ATTN_WIDTH:3 * ATTN_WIDTH]

    su0 = 3 * ATTN_WIDTH
    sv0 = su0 + SGU_WIDTH
    row = lax.broadcasted_iota(jnp.int32, (SGU_CHUNK, SGU_CHUNK), 0)
    col = lax.broadcasted_iota(jnp.int32, (SGU_CHUNK, SGU_CHUNK), 1)
    causal = col <= row
    left = lax.broadcasted_iota(jnp.int32, (SGU_CHUNK, LANES), 1) < HEAD_DIM
    gated_cols = []
    for p in range(SGU_WIDTH // LANES):
        cs = slice(p * LANES, (p + 1) * LANES)
        svn = (_group_rms(proj[:, sv0 + p * LANES:sv0 + (p + 1) * LANES], bd) * sg_ref[:, cs]).astype(BF16)
        w0 = jnp.where(causal, wsgu_ref[2 * p], 0.0).astype(BF16)
        w1 = jnp.where(causal, wsgu_ref[2 * p + 1], 0.0).astype(BF16)
        chunks = []
        for c in range(TM // SGU_CHUNK):
            blk = svn[c * SGU_CHUNK:(c + 1) * SGU_CHUNK, :]
            r0 = jnp.dot(w0, blk, preferred_element_type=F32)
            r1 = jnp.dot(w1, blk, preferred_element_type=F32)
            chunks.append(jnp.where(left, r0, r1) + bsgu_ref[:, cs])
        mixed = jnp.concatenate(chunks, axis=0)
        gated_cols.append(proj[:, su0 + p * LANES:su0 + (p + 1) * LANES] * mixed)
    gated = jnp.concatenate(gated_cols, axis=1)
    gn_ref[...] = _rms(gated, sog_ref[...])


def _in_proj(x, g1, win, qg, kg, sg, bd, wsgu, bsgu, sog):
    full = lambda *shape: pl.BlockSpec(shape, lambda i: (0,) * len(shape))
    tile = lambda width: pl.BlockSpec((TM, width), lambda i: (i, 0))
    out = jax.ShapeDtypeStruct((TOKENS, ATTN_WIDTH), F32)
    return pl.pallas_call(
        _in_proj_kernel,
        grid=(TOKENS // TM,),
        in_specs=[tile(D_MODEL), full(1, D_MODEL), full(D_MODEL, IN_WIDTH), full(1, ATTN_WIDTH), full(1, ATTN_WIDTH),
                  full(1, SGU_WIDTH), full(LANES, LANES), full(SGU_GROUPS, SGU_CHUNK, SGU_CHUNK),
                  full(SGU_CHUNK, SGU_WIDTH), full(1, SGU_WIDTH)],
        out_specs=[tile(ATTN_WIDTH)] * 4,
        out_shape=[out] * 4,
        compiler_params=pltpu.CompilerParams(dimension_semantics=("parallel",), vmem_limit_bytes=VMEM_LIMIT),
        name="in_proj",
    )(x, g1, win, qg, kg, sg, bd, wsgu, bsgu, sog)


def _attn_kernel(slopes_ref, q_ref, k_ref, v_ref, o_ref, *scr):
    hp = pl.program_id(1)
    left = lax.broadcasted_iota(jnp.int32, (ATTN_BLOCK, LANES), 1) < HEAD_DIM
    qi = lax.broadcasted_iota(jnp.int32, (ATTN_BLOCK, 2 * ATTN_BLOCK), 0)
    ki = lax.broadcasted_iota(jnp.int32, (ATTN_BLOCK, 2 * ATTN_BLOCK), 1)
    base = ATTN_BLOCK + qi - ki

    for br, (window, d) in enumerate(DILATED_BRANCHES):
        nblk = SEQ // d // ATTN_BLOCK
        w_sub = window // d
        o_scr, m_scr, l_scr = scr[3 * br:3 * br + 3]

        def rows(start, size):
            return pl.ds(start, size) if d == 1 else pl.ds(start, size, stride=d)

        def block(idx):
            r = idx // nblk
            n = idx % nblk
            first = jnp.maximum(n - 1, 0)
            qs = d * ATTN_BLOCK * n + r
            ks = d * ATTN_BLOCK * first + r
            qb = q_ref[0, rows(qs, ATTN_BLOCK), :]
            kb = k_ref[0, rows(ks, 2 * ATTN_BLOCK), :].astype(BF16)
            vb = v_ref[0, rows(ks, 2 * ATTN_BLOCK), :].astype(BF16)
            dist = base - ATTN_BLOCK * (1 - (n - first))
            valid = (dist >= 0) & (dist <= w_sub)
            distf = (d * dist).astype(F32)
            res = []
            for hh in range(2):
                slope = slopes_ref[2 * hp + hh]
                qh = jnp.where(left if hh == 0 else jnp.logical_not(left), qb, 0.0).astype(BF16)
                s = lax.dot_general(qh, kb, (((1,), (1,)), ((), ())), preferred_element_type=F32)
                s = jnp.where(valid, s * (HEAD_DIM ** -0.5) - slope * distf, NEG_INF)
                m = jnp.max(s, axis=-1, keepdims=True)
                p = jnp.exp(s - m)
                l = jnp.sum(p, axis=-1, keepdims=True)
                o = jnp.dot(p.astype(BF16), vb, preferred_element_type=F32)
                res.append((o, m, l))
            dst = rows(qs, ATTN_BLOCK)
            o_scr[dst, :] = jnp.where(left, res[0][0], res[1][0])
            m_scr[dst, :] = jnp.where(left, res[0][1], res[1][1])
            l_scr[dst, :] = jnp.where(left, res[0][2], res[1][2])

        def body(g, carry):
            for u in range(ATTN_UNROLL):
                block(g * ATTN_UNROLL + u)
            return carry

        lax.fori_loop(0, SEQ // ATTN_BLOCK // ATTN_UNROLL, body, 0)

    (o1, m1, l1), (o2, m2, l2), (o3, m3, l3) = [tuple(s[...] for s in scr[3 * b:3 * b + 3]) for b in range(3)]
    m_max = jnp.maximum(jnp.maximum(m1, m2), m3)
    w1 = jnp.exp(m1 - m_max)
    w2 = jnp.exp(m2 - m_max)
    w3 = jnp.exp(m3 - m_max)
    num = w1 * o1 + w2 * o2 + w3 * o3
    den = w1 * l1 + w2 * l2 + w3 * l3
    o_ref[0] = num / den


def _attention(slopes, qn, kn, v):
    blk = pl.BlockSpec((1, SEQ, LANES), lambda b, hp: (b, 0, hp))
    shape3 = (BATCH, SEQ, ATTN_WIDTH)
    return pl.pallas_call(
        _attn_kernel,
        grid=(BATCH, ATTN_WIDTH // LANES),
        in_specs=[pl.BlockSpec(memory_space=pltpu.SMEM), blk, blk, blk],
        out_specs=blk,
        out_shape=jax.ShapeDtypeStruct(shape3, F32),
        scratch_shapes=[pltpu.VMEM((SEQ, LANES), F32)] * 9,
        compiler_params=pltpu.CompilerParams(dimension_semantics=("parallel", "parallel"),
                                             vmem_limit_bytes=VMEM_LIMIT),
        name="attention",
    )(slopes, qn.reshape(shape3), kn.reshape(shape3), v.reshape(shape3)).reshape(TOKENS, ATTN_WIDTH)


def _out_proj_kernel(attn_ref, gn_ref, x_ref, ag_ref, wout_ref, g2_ref, wq_ref, x1_ref, h2_ref, qp_ref):
    an = _rms(attn_ref[...], ag_ref[...])
    mixed = (jnp.dot(an.astype(BF16), wout_ref[:ATTN_WIDTH, :], preferred_element_type=F32)
             + jnp.dot(gn_ref[...].astype(BF16), wout_ref[ATTN_WIDTH:, :], preferred_element_type=F32))
    x1 = x_ref[...] + mixed
    x1_ref[...] = x1
    h2 = _rms(x1, g2_ref[...])
    h2_ref[...] = h2
    qp_ref[...] = jnp.dot(h2.astype(BF16), wq_ref[...], preferred_element_type=F32)


def _out_proj(attn, gn, x, ag, wout, g2, wq):
    full = lambda *shape: pl.BlockSpec(shape, lambda i: (0,) * len(shape))
    tile = lambda width: pl.BlockSpec((TM, width), lambda i: (i, 0))
    qw = wq.shape[1]
    return pl.pallas_call(
        _out_proj_kernel,
        grid=(TOKENS // TM,),
        in_specs=[tile(ATTN_WIDTH), tile(SGU_WIDTH), tile(D_MODEL), full(1, ATTN_WIDTH), full(D_MODEL, D_MODEL),
                  full(1, D_MODEL), full(D_MODEL, qw)],
        out_specs=[tile(D_MODEL), tile(D_MODEL), tile(qw)],
        out_shape=[jax.ShapeDtypeStruct((TOKENS, D_MODEL), F32), jax.ShapeDtypeStruct((TOKENS, D_MODEL), F32),
                   jax.ShapeDtypeStruct((TOKENS, qw), F32)],
        compiler_params=pltpu.CompilerParams(dimension_semantics=("parallel",), vmem_limit_bytes=VMEM_LIMIT),
        name="out_proj",
    )(attn, gn, x, ag, wout, g2, wq)


def _top16(vals, ids, n_rows):
    pos = lax.broadcasted_iota(jnp.int32, (n_rows, TB), 0)
    best, picked = [], []
    for _ in range(PEER_TOPK):
        m = jnp.max(vals, axis=0, keepdims=True)
        p = jnp.min(jnp.where(vals == m, pos, n_rows), axis=0, keepdims=True)
        sel = pos == p
        best.append(m)
        picked.append(p if ids is None else jnp.max(jnp.where(sel, ids, -1), axis=0, keepdims=True))
        vals = jnp.where(sel, -jnp.inf, vals)
    return jnp.concatenate(best, axis=0), jnp.concatenate(picked, axis=0)


def _candidates(v1, i1, v2, i2):
    sub = lax.broadcasted_iota(jnp.int32, (8, TB), 0)
    vals = [v1[0:1] + v2]
    ids = [i1[0:1] * N_SUB_KEYS + i2]
    for a in range(1, 8):
        live = PEER_TOPK // (a + 1)
        c = v1[a:a + 1] + v2[0:8]
        vals.append(c if live >= 8 else jnp.where(sub < live, c, -jnp.inf))
        ids.append(i1[a:a + 1] * N_SUB_KEYS + i2[0:8])
    vals.append(v1[8:16] + v2[0:1])
    ids.append(i1[8:16] * N_SUB_KEYS + i2[0:1])
    return jnp.concatenate(vals, axis=0), jnp.concatenate(ids, axis=0)


def _topk_kernel(qp_ref, sk_ref, eidx_ref, gates_ref):
    def head(h):
        tops = []
        for half in range(2):
            c0 = pl.multiple_of((2 * h + half) * N_SUB_KEYS, N_SUB_KEYS)
            qh = qp_ref[:, pl.ds(c0, N_SUB_KEYS)].astype(BF16)
            s = lax.dot_general(sk_ref[half], qh, (((1,), (1,)), ((), ())), preferred_element_type=F32)
            tops.append(_top16(s, None, N_SUB_KEYS))
        (v1, i1), (v2, i2) = tops
        cand, cidx = _candidates(v1, i1, v2, i2)
        best, eid = _top16(cand, cidx, cand.shape[0])
        ex = jnp.exp(best - best[0:1, :])
        rs = pl.ds(pl.multiple_of(h * PEER_TOPK, PEER_TOPK), PEER_TOPK)
        gates_ref[0, rs, :] = ex / jnp.sum(ex, axis=0, keepdims=True)
        odd_slot = lax.broadcasted_iota(jnp.int32, (PEER_TOPK, TB), 0) % 2
        eidx_ref[0, rs, :] = eid * ROWS_PER_EXPERT + TABLE_PAD - ROWS_PER_EXPERT * odd_slot

    def body(g, carry):
        for u in range(TOPK_UNROLL):
            head(g * TOPK_UNROLL + u)
        return carry

    lax.fori_loop(0, PEER_HEADS // TOPK_UNROLL, body, 0)


def _peer_topk(qp, sk):
    nb = TOKENS // TB
    qw = qp.shape[1]
    out_blk = pl.BlockSpec((1, HK, TB), lambda i: (i, 0, 0))
    return pl.pallas_call(
        _topk_kernel,
        grid=(nb,),
        in_specs=[pl.BlockSpec((TB, qw), lambda i: (i, 0)),
                  pl.BlockSpec((2, N_SUB_KEYS, N_SUB_KEYS), lambda i: (0, 0, 0))],
        out_specs=[out_blk, out_blk],
        out_shape=[jax.ShapeDtypeStruct((nb, HK, TB), jnp.int32), jax.ShapeDtypeStruct((nb, HK, TB), F32)],
        compiler_params=pltpu.CompilerParams(dimension_semantics=("parallel",), vmem_limit_bytes=VMEM_LIMIT),
        name="peer_topk",
    )(qp, sk)


def _pack_table(w):
    wb = w.astype(BF16)
    half = D_MODEL // 2
    pairs = jnp.stack([wb[:, :half], wb[:, half:]], axis=-1)
    rows = lax.bitcast_convert_type(pairs, jnp.int32).reshape(N_EXPERTS * ROWS_PER_EXPERT, LANES)
    return jnp.pad(rows, ((TABLE_PAD, TABLE_PAD), (0, 0)))


def _unpack(w):
    return (pltpu.unpack_elementwise(w, index=0, packed_dtype=BF16, unpacked_dtype=F32),
            pltpu.unpack_elementwise(w, index=1, packed_dtype=BF16, unpacked_dtype=F32))


def _gather_rows(tab, idx_smem, stage, off, paired):
    sub = lax.broadcasted_iota(jnp.int32, (8, LANES), 0)
    for p in range(HK // 2):
        rows = [pl.multiple_of(idx_smem.at[pl.ds(e * TB, (HK + 1) * TB)][off], ROWS_PER_EXPERT)
                for e in (2 * p, 2 * p + 1)]
        if paired:
            even, odd = (tab[pl.ds(r, 8), :] for r in rows)
            stage[8 * p:8 * (p + 1), :] = jnp.where(sub < ROWS_PER_EXPERT, even, odd)
        else:
            stage[8 * p:8 * p + 4, :] = tab[pl.ds(rows[0], ROWS_PER_EXPERT), :]
            stage[8 * p + 4:8 * (p + 1), :] = tab[pl.ds(rows[1] + ROWS_PER_EXPERT, ROWS_PER_EXPERT), :]


def _start_block(i, n, idx_hbm, tab_hbm, tab, idx_smem, tab_sem, idx_sem):
    slot = lax.rem(i, 2)

    def idx_copy(blk, s):
        return pltpu.make_async_copy(idx_hbm.at[blk], idx_smem.at[pl.ds(s * (HK * TB), HK * TB)], idx_sem.at[s])

    @pl.when(i == 0)
    def _():
        idx_copy(0, 0).start()
        cp = pltpu.make_async_copy(tab_hbm, tab, tab_sem)
        cp.start()
        cp.wait()

    idx_copy(i, slot).wait()

    @pl.when(i + 1 < n)
    def _():
        idx_copy(i + 1, 1 - slot).start()

    return slot * (HK * TB)


def _expert_groups(stage):
    for j in range(HK // 8):
        yield j, [_unpack(stage[pl.ds(8 * ROWS_PER_EXPERT * j + s, 8, stride=ROWS_PER_EXPERT), :])
                  for s in range(ROWS_PER_EXPERT)]


def _peer_u_kernel(eidx_hbm, utab_hbm, h_ref, gates_ref, coef_ref, tab, idx_smem, *scratch):
    stages = scratch[:N_STAGE]
    prods, a_scr, tab_sem, idx_sem = scratch[N_STAGE:]
    base = _start_block(pl.program_id(0), pl.num_programs(0), eidx_hbm, utab_hbm, tab, idx_smem, tab_sem, idx_sem)
    lane = lax.broadcasted_iota(jnp.int32, (HK, TB), 1)
    a_scr[...] = jnp.zeros((HK, TB), F32)
    prods[0] = jnp.zeros((HK, LANES), F32)

    def products(stage, t):
        h = h_ref[t]
        hb = [jnp.broadcast_to(h[r:r + 1, :], (8, LANES)) for r in range(2 * ROWS_PER_EXPERT)]
        for j, rows in _expert_groups(stage):
            acc = None
            for s, (lo, hi) in enumerate(rows):
                term = lo * hb[s] + hi * hb[ROWS_PER_EXPERT + s]
                acc = term if acc is None else acc + term
            prods[t + 1, 8 * j:8 * (j + 1), :] = acc

    def reduce(t):
        col = jnp.sum(prods[t + 1], axis=1, keepdims=True)
        a_scr[...] = jnp.where(lane == t, col, a_scr[...])

    for t in range(GATHER_AHEAD):
        _gather_rows(tab, idx_smem, stages[t], base + t, paired=True)

    def ring(k, carry):
        for u in range(N_STAGE):
            t = N_STAGE * k + u
            _gather_rows(tab, idx_smem, stages[(u + GATHER_AHEAD) % N_STAGE],
                         base + jnp.minimum(t + GATHER_AHEAD, TB - 1), paired=True)
            products(stages[u], t)
            reduce(t - 1)
        return carry

    lax.fori_loop(0, TB // N_STAGE, ring, 0)
    reduce(TB - 1)
    a = a_scr[...]
    gelu = 0.5 * a * (1.0 + lax.erf(a * (2.0 ** -0.5)))
    coef_ref[0] = gates_ref[0] * gelu


def _peer_v_kernel(eidx_hbm, vtab_hbm, coef_ref, x1_ref, o_ref, tab, idx_smem, *scratch):
    stages = scratch[:N_STAGE]
    cbufs = scratch[N_STAGE:2 * N_STAGE]
    tab_sem, idx_sem = scratch[2 * N_STAGE:]
    base = _start_block(pl.program_id(0), pl.num_programs(0), eidx_hbm, vtab_hbm, tab, idx_smem, tab_sem, idx_sem)
    lane = lax.broadcasted_iota(jnp.int32, (HK, TB), 1)

    def fill(stage, cbuf, t):
        _gather_rows(tab, idx_smem, stage, base + t, paired=False)
        col = jnp.sum(jnp.where(lane == t, coef_ref[0], 0.0), axis=1, keepdims=True)
        cbuf[...] = jnp.broadcast_to(col, (HK, LANES))

    def weighted_sum(stage, cbuf, t):
        lo_acc = [None] * ROWS_PER_EXPERT
        hi_acc = [None] * ROWS_PER_EXPERT
        for j, rows in _expert_groups(stage):
            cb = cbuf[8 * j:8 * (j + 1), :]
            for s, (lo, hi) in enumerate(rows):
                lo_acc[s] = lo * cb if lo_acc[s] is None else lo_acc[s] + lo * cb
                hi_acc[s] = hi * cb if hi_acc[s] is None else hi_acc[s] + hi * cb
        rows = [jnp.sum(a, axis=0, keepdims=True) for a in lo_acc + hi_acc]
        o_ref[t] = x1_ref[t] + jnp.concatenate(rows, axis=0)

    for t in range(GATHER_AHEAD):
        fill(stages[t], cbufs[t], t)

    def ring(k, carry):
        for u in range(N_STAGE):
            t = N_STAGE * k + u
            nxt = (u + GATHER_AHEAD) % N_STAGE
            weighted_sum(stages[u], cbufs[u], t)
            fill(stages[nxt], cbufs[nxt], jnp.minimum(t + GATHER_AHEAD, TB - 1))
        return carry

    lax.fori_loop(0, TB // N_STAGE, ring, 0)


def _peer_gather(eidx, gates, h2, x1, utab, vtab):
    nb = TOKENS // TB
    eidx_flat = eidx.reshape(nb, HK * TB)
    any_spec = pl.BlockSpec(memory_space=pl.ANY)
    tok_blk = pl.BlockSpec((TB, 8, LANES), lambda i: (i, 0, 0))
    col_blk = pl.BlockSpec((1, HK, TB), lambda i: (i, 0, 0))
    table = pltpu.VMEM((N_EXPERTS * ROWS_PER_EXPERT + 2 * TABLE_PAD, LANES), jnp.int32)
    idx_scr = pltpu.SMEM((2 * HK * TB,), jnp.int32)
    stage = pltpu.VMEM((HK * ROWS_PER_EXPERT, LANES), jnp.int32)
    params = pltpu.CompilerParams(dimension_semantics=("arbitrary",), vmem_limit_bytes=VMEM_LIMIT)
    coef = pl.pallas_call(
        _peer_u_kernel,
        grid=(nb,),
        in_specs=[any_spec, any_spec, tok_blk, col_blk],
        out_specs=col_blk,
        out_shape=jax.ShapeDtypeStruct((nb, HK, TB), F32),
        scratch_shapes=[table, idx_scr] + [stage] * N_STAGE + [
            pltpu.VMEM((TB + 1, HK, LANES), F32), pltpu.VMEM((HK, TB), F32),
            pltpu.SemaphoreType.DMA, pltpu.SemaphoreType.DMA((2,))],
        compiler_params=params,
        name="peer_u",
    )(eidx_flat, utab, h2.reshape(TOKENS, 8, LANES), gates)
    return pl.pallas_call(
        _peer_v_kernel,
        grid=(nb,),
        in_specs=[any_spec, any_spec, col_blk, tok_blk],
        out_specs=tok_blk,
        out_shape=jax.ShapeDtypeStruct((TOKENS, 8, LANES), F32),
        scratch_shapes=[table, idx_scr] + [stage] * N_STAGE + [pltpu.VMEM((HK, LANES), F32)] * N_STAGE + [
            pltpu.SemaphoreType.DMA, pltpu.SemaphoreType.DMA((2,))],
        compiler_params=params,
        name="peer_v",
    )(eidx_flat, vtab, coef, x1.reshape(TOKENS, 8, LANES))


def kernel(x, norm1_g, w_in, q_norm_g, k_norm_g, sgu_norm_g, sgu_w, sgu_b, attn_out_g, sgu_out_g, w_out, norm2_g,
           w_query, sub_keys, expert_u, expert_v):
    b, s, _ = x.shape
    xf = x.reshape(TOKENS, D_MODEL)
    i = jnp.arange(1, ATTN_HEADS + 1, dtype=F32)
    slopes = jnp.exp2(-8.0 * i / ATTN_HEADS)
    g = lax.broadcasted_iota(jnp.int32, (LANES, LANES), 0) // HEAD_DIM
    bd = (g == g.T).astype(BF16)
    for layer in range(norm1_g.shape[0]):
        qn, kn, v, gn = _in_proj(
            xf, norm1_g[layer][None], w_in[layer].astype(BF16),
            jnp.tile(q_norm_g[layer], ATTN_HEADS)[None], jnp.tile(k_norm_g[layer], ATTN_HEADS)[None],
            sgu_norm_g[layer][None], bd, sgu_w[layer],
            jnp.repeat(sgu_b[layer].T, HEAD_DIM, axis=1), sgu_out_g[layer][None])
        attn = _attention(slopes, qn, kn, v)
        x1, h2, qp = _out_proj(attn, gn, xf, attn_out_g[layer][None], w_out[layer].astype(BF16),
                               norm2_g[layer][None], w_query[layer].astype(BF16))
        eidx, gates = _peer_topk(qp, sub_keys[layer].astype(BF16))
        xf = _peer_gather(eidx, gates, h2, x1, _pack_table(expert_u[layer]),
                          _pack_table(expert_v[layer])).reshape(TOKENS, D_MODEL)
    return xf.reshape(b, s, D_MODEL)
```

```python
import functools

import jax
import jax.numpy as jnp
from jax import lax
from jax.experimental import pallas as pl
from jax.experimental.pallas import tpu as pltpu

F32 = jnp.float32
BF16 = jnp.bfloat16

D_MODEL = 1024
BATCH = 8
SEQ = 4096
TOKENS = BATCH * SEQ
ATTN_HEADS = 8
HEAD_DIM = 64
ATTN_WIDTH = 512
SGU_GROUPS = 8
SGU_WIDTH = 512
SGU_CHUNK = 128
IN_WIDTH = 3 * ATTN_WIDTH + 2 * SGU_WIDTH
DILATED_BRANCHES = ((128, 1), (512, 4), (2048, 16))
ATTN_BLOCK = 128
ATTN_UNROLL = 8
PEER_HEADS = 8
N_SUB_KEYS = 128
N_EXPERTS = N_SUB_KEYS * N_SUB_KEYS
PEER_TOPK = 16
TOPK_UNROLL = 8
HK = PEER_HEADS * PEER_TOPK
U_STAGES = 2
V_STAGES = 4
GATHER_AHEAD = 1
EPS = 1e-6
NEG_INF = -1e30

LANES = 128
TM = 512
TB = 128
ROWS_PER_EXPERT = 4
TABLE_PAD = 8
PACK_TILE = 512
VMEM_LIMIT = 56 * 1024 * 1024


def _rms(x, gain):
    return x * lax.rsqrt(jnp.mean(x * x, axis=-1, keepdims=True) + EPS) * gain


def _group_rms(t, bd):
    sq = t * t
    hi = sq.astype(BF16)
    lo = (sq - hi.astype(F32)).astype(BF16)
    ssq = jnp.dot(hi, bd, preferred_element_type=F32) + jnp.dot(lo, bd, preferred_element_type=F32)
    return t * lax.rsqrt(ssq * (1.0 / HEAD_DIM) + EPS)


def _in_proj_kernel(x_ref, g1_ref, win_ref, qg_ref, kg_ref, sg_ref, bd_ref, wsgu_ref, bsgu_ref, sog_ref,
                    qn_ref, kn_ref, v_ref, gn_ref):
    h = _rms(x_ref[...], g1_ref[...])
    proj = jnp.dot(h.astype(BF16), win_ref[...], preferred_element_type=F32)
    bd = bd_ref[...]
    for c in range(ATTN_WIDTH // LANES):
        cs = slice(c * LANES, (c + 1) * LANES)
        qn_ref[:, cs] = _group_rms(proj[:, c * LANES:(c + 1) * LANES], bd) * qg_ref[:, cs]
        kn_ref[:, cs] = _group_rms(proj[:, ATTN_WIDTH + c * LANES:ATTN_WIDTH + (c + 1) * LANES], bd) * kg_ref[:, cs]
    v_ref[...] = proj[:, 2 * ATTN_WIDTH:3 * ATTN_WIDTH]

    su0 = 3 * ATTN_WIDTH
    sv0 = su0 + SGU_WIDTH
    row = lax.broadcasted_iota(jnp.int32, (SGU_CHUNK, SGU_CHUNK), 0)
    col = lax.broadcasted_iota(jnp.int32, (SGU_CHUNK, SGU_CHUNK), 1)
    causal = col <= row
    left = lax.broadcasted_iota(jnp.int32, (SGU_CHUNK, LANES), 1) < HEAD_DIM
    gated_cols = []
    for p in range(SGU_WIDTH // LANES):
        cs = slice(p * LANES, (p + 1) * LANES)
        svn = (_group_rms(proj[:, sv0 + p * LANES:sv0 + (p + 1) * LANES], bd) * sg_ref[:, cs]).astype(BF16)
        w0 = jnp.where(causal, wsgu_ref[2 * p], 0.0).astype(BF16)
        w1 = jnp.where(causal, wsgu_ref[2 * p + 1], 0.0).astype(BF16)
        chunks = []
        for c in range(TM // SGU_CHUNK):
            blk = svn[c * SGU_CHUNK:(c + 1) * SGU_CHUNK, :]
            r0 = jnp.dot(w0, blk, preferred_element_type=F32)
            r1 = jnp.dot(w1, blk, preferred_element_type=F32)
            chunks.append(jnp.where(left, r0, r1) + bsgu_ref[:, cs])
        mixed = jnp.concatenate(chunks, axis=0)
        gated_cols.append(proj[:, su0 + p * LANES:su0 + (p + 1) * LANES] * mixed)
    gated = jnp.concatenate(gated_cols, axis=1)
    gn_ref[...] = _rms(gated, sog_ref[...])


def _in_proj(x, g1, win, qg, kg, sg, bd, wsgu, bsgu, sog):
    full = lambda *shape: pl.BlockSpec(shape, lambda i: (0,) * len(shape))
    tile = lambda width: pl.BlockSpec((TM, width), lambda i: (i, 0))
    out = jax.ShapeDtypeStruct((TOKENS, ATTN_WIDTH), F32)
    return pl.pallas_call(
        _in_proj_kernel,
        grid=(TOKENS // TM,),
        in_specs=[tile(D_MODEL), full(1, D_MODEL), full(D_MODEL, IN_WIDTH), full(1, ATTN_WIDTH), full(1, ATTN_WIDTH),
                  full(1, SGU_WIDTH), full(LANES, LANES), full(SGU_GROUPS, SGU_CHUNK, SGU_CHUNK),
                  full(SGU_CHUNK, SGU_WIDTH), full(1, SGU_WIDTH)],
        out_specs=[tile(ATTN_WIDTH)] * 4,
        out_shape=[out] * 4,
        compiler_params=pltpu.CompilerParams(dimension_semantics=("parallel",), vmem_limit_bytes=VMEM_LIMIT),
        name="in_proj",
    )(x, g1, win, qg, kg, sg, bd, wsgu, bsgu, sog)


def _attn_kernel(slopes_ref, q_ref, k_ref, v_ref, o_ref, *scr):
    hp = pl.program_id(1)
    bias_scr = scr[-1]
    left = lax.broadcasted_iota(jnp.int32, (ATTN_BLOCK, LANES), 1) < HEAD_DIM
    left_kv = lax.broadcasted_iota(jnp.int32, (2 * ATTN_BLOCK, LANES), 1) < HEAD_DIM
    qi = lax.broadcasted_iota(jnp.int32, (ATTN_BLOCK, 2 * ATTN_BLOCK), 0)
    ki = lax.broadcasted_iota(jnp.int32, (ATTN_BLOCK, 2 * ATTN_BLOCK), 1)
    base = ATTN_BLOCK + qi - ki

    for br, (window, d) in enumerate(DILATED_BRANCHES):
        nblk = SEQ // d // ATTN_BLOCK
        w_sub = window // d
        o_scr, m_scr, mx_scr, lx_scr = scr[4 * br:4 * br + 4]

        for later in range(2):
            dist = base - ATTN_BLOCK * (1 - later)
            valid = (dist >= 0) & (dist <= w_sub)
            distf = (d * dist).astype(F32)
            for hh in range(2):
                bias_scr[hh, later] = jnp.where(valid, -slopes_ref[2 * hp + hh] * distf, NEG_INF)

        def rows(start, size):
            return pl.ds(start, size) if d == 1 else pl.ds(start, size, stride=d)

        def block(idx):
            r = idx // nblk
            n = idx % nblk
            later = jnp.minimum(n, 1)
            qs = d * ATTN_BLOCK * n + r
            ks = d * ATTN_BLOCK * (n - later) + r
            qb = q_ref[0, rows(qs, ATTN_BLOCK), :] * (HEAD_DIM ** -0.5)
            kb = k_ref[0, rows(ks, 2 * ATTN_BLOCK), :].astype(BF16)
            vb = v_ref[0, rows(ks, 2 * ATTN_BLOCK), :]
            res = []
            for hh in range(2):
                mine = left if hh == 0 else jnp.logical_not(left)
                mine_kv = left_kv if hh == 0 else jnp.logical_not(left_kv)
                qh = jnp.where(mine, qb, 0.0).astype(BF16)
                s = lax.dot_general(qh, kb, (((1,), (1,)), ((), ())), preferred_element_type=F32)
                s = s + bias_scr[hh, later]
                m = jnp.max(jnp.maximum(s[:, :ATTN_BLOCK], s[:, ATTN_BLOCK:]), axis=-1, keepdims=True)
                p = jnp.exp(s - m).astype(BF16)
                ol = jnp.dot(p, jnp.where(mine_kv, vb, 1.0).astype(BF16), preferred_element_type=F32)
                res.append((ol, m))
            dst = rows(qs, ATTN_BLOCK)
            o_scr[dst, :] = jnp.where(left, res[0][0], res[1][0])
            lx_scr[dst, :] = jnp.where(left, res[1][0], res[0][0])
            m_scr[dst, :] = jnp.where(left, res[0][1], res[1][1])
            mx_scr[dst, :] = jnp.where(left, res[1][1], res[0][1])

        def body(g, carry):
            for u in range(ATTN_UNROLL):
                block(g * ATTN_UNROLL + u)
            return carry

        lax.fori_loop(0, SEQ // ATTN_BLOCK // ATTN_UNROLL, body, 0)

    o_b, m_b, mx_b, lx_b = [[scr[4 * b + i][...] for b in range(3)] for i in range(4)]
    m_max = jnp.maximum(jnp.maximum(m_b[0], m_b[1]), m_b[2])
    num = sum(jnp.exp(m - m_max) * o for m, o in zip(m_b, o_b))
    mx_max = jnp.maximum(jnp.maximum(mx_b[0], mx_b[1]), mx_b[2])
    den_x = sum(jnp.exp(m - mx_max) * l for m, l in zip(mx_b, lx_b))
    o_ref[0] = num / pltpu.roll(den_x, HEAD_DIM, axis=1)


def _attention(slopes, qn, kn, v):
    blk = pl.BlockSpec((1, SEQ, LANES), lambda b, hp: (b, 0, hp))
    shape3 = (BATCH, SEQ, ATTN_WIDTH)
    return pl.pallas_call(
        _attn_kernel,
        grid=(BATCH, ATTN_WIDTH // LANES),
        in_specs=[pl.BlockSpec(memory_space=pltpu.SMEM), blk, blk, blk],
        out_specs=blk,
        out_shape=jax.ShapeDtypeStruct(shape3, F32),
        scratch_shapes=[pltpu.VMEM((SEQ, LANES), F32)] * 12 + [
            pltpu.VMEM((2, 2, ATTN_BLOCK, 2 * ATTN_BLOCK), F32)],
        compiler_params=pltpu.CompilerParams(dimension_semantics=("parallel", "parallel"),
                                             vmem_limit_bytes=VMEM_LIMIT),
        name="attention",
    )(slopes, qn.reshape(shape3), kn.reshape(shape3), v.reshape(shape3)).reshape(TOKENS, ATTN_WIDTH)


def _out_proj_kernel(attn_ref, gn_ref, x_ref, ag_ref, wout_ref, g2_ref, wq_ref, x1_ref, h2_ref, qp_ref):
    an = _rms(attn_ref[...], ag_ref[...])
    mixed = (jnp.dot(an.astype(BF16), wout_ref[:ATTN_WIDTH, :], preferred_element_type=F32)
             + jnp.dot(gn_ref[...].astype(BF16), wout_ref[ATTN_WIDTH:, :], preferred_element_type=F32))
    x1 = x_ref[...] + mixed
    h2 = _rms(x1, g2_ref[...])
    for s in range(D_MODEL // LANES):
        x1_ref[:, s, :] = x1[:, s * LANES:(s + 1) * LANES]
        h2_ref[:, s, :] = h2[:, s * LANES:(s + 1) * LANES]
    qp_ref[...] = jnp.dot(h2.astype(BF16), wq_ref[...], preferred_element_type=F32)


def _out_proj(attn, gn, x, ag, wout, g2, wq):
    full = lambda *shape: pl.BlockSpec(shape, lambda i: (0,) * len(shape))
    tile = lambda width: pl.BlockSpec((TM, width), lambda i: (i, 0))
    qw = wq.shape[1]
    rows = pl.BlockSpec((TM, D_MODEL // LANES, LANES), lambda i: (i, 0, 0))
    rows_shape = jax.ShapeDtypeStruct((TOKENS, D_MODEL // LANES, LANES), F32)
    return pl.pallas_call(
        _out_proj_kernel,
        grid=(TOKENS // TM,),
        in_specs=[tile(ATTN_WIDTH), tile(SGU_WIDTH), tile(D_MODEL), full(1, ATTN_WIDTH), full(D_MODEL, D_MODEL),
                  full(1, D_MODEL), full(D_MODEL, qw)],
        out_specs=[rows, rows, tile(qw)],
        out_shape=[rows_shape, rows_shape, jax.ShapeDtypeStruct((TOKENS, qw), F32)],
        compiler_params=pltpu.CompilerParams(dimension_semantics=("parallel",), vmem_limit_bytes=VMEM_LIMIT),
        name="out_proj",
    )(attn, gn, x, ag, wout, g2, wq)


def _top16(vals, ids, n_rows):
    pos = lax.broadcasted_iota(jnp.int32, (n_rows, TB), 0).astype(F32)
    best, picked = [], []
    for _ in range(PEER_TOPK):
        m = jnp.max(vals, axis=0, keepdims=True)
        p = jnp.min(jnp.where(vals == m, pos, float(n_rows)), axis=0, keepdims=True)
        sel = pos == p
        best.append(m)
        picked.append(p if ids is None else jnp.max(jnp.where(sel, ids, -1.0), axis=0, keepdims=True))
        vals = jnp.where(sel, -jnp.inf, vals)
    return jnp.concatenate(best, axis=0), jnp.concatenate(picked, axis=0)


def _candidates(v1, i1, v2, i2):
    sub = lax.broadcasted_iota(jnp.int32, (8, TB), 0)
    vals = [v1[0:1] + v2]
    ids = [i1[0:1] * N_SUB_KEYS + i2]
    for a in range(1, 8):
        live = PEER_TOPK // (a + 1)
        c = v1[a:a + 1] + v2[0:8]
        vals.append(c if live >= 8 else jnp.where(sub < live, c, -jnp.inf))
        ids.append(i1[a:a + 1] * N_SUB_KEYS + i2[0:8])
    vals.append(v1[8:16] + v2[0:1])
    ids.append(i1[8:16] * N_SUB_KEYS + i2[0:1])
    return jnp.concatenate(vals, axis=0), jnp.concatenate(ids, axis=0)


def _topk_kernel(qp_ref, sk_ref, eidx_ref, gates_ref):
    def head(h):
        tops = []
        for half in range(2):
            c0 = pl.multiple_of((2 * h + half) * N_SUB_KEYS, N_SUB_KEYS)
            qh = qp_ref[:, pl.ds(c0, N_SUB_KEYS)].astype(BF16)
            s = lax.dot_general(sk_ref[half], qh, (((1,), (1,)), ((), ())), preferred_element_type=F32)
            tops.append(_top16(s, None, N_SUB_KEYS))
        (v1, i1), (v2, i2) = tops
        cand, cidx = _candidates(v1, i1, v2, i2)
        best, eid = _top16(cand, cidx, cand.shape[0])
        ex = jnp.exp(best - best[0:1, :])
        rs = pl.ds(pl.multiple_of(h * PEER_TOPK, PEER_TOPK), PEER_TOPK)
        gates_ref[0, rs, :] = ex / jnp.sum(ex, axis=0, keepdims=True)
        odd_slot = lax.broadcasted_iota(jnp.int32, (PEER_TOPK, TB), 0) % 2
        row = eid.astype(jnp.int32) * ROWS_PER_EXPERT + TABLE_PAD - ROWS_PER_EXPERT * odd_slot
        eidx_ref[0, rs, :] = row

    def body(g, carry):
        for u in range(TOPK_UNROLL):
            head(g * TOPK_UNROLL + u)
        return carry

    lax.fori_loop(0, PEER_HEADS // TOPK_UNROLL, body, 0)


def _peer_topk(qp, sk):
    nb = TOKENS // TB
    qw = qp.shape[1]
    out_blk = pl.BlockSpec((1, HK, TB), lambda i: (i, 0, 0))
    return pl.pallas_call(
        _topk_kernel,
        grid=(nb,),
        in_specs=[pl.BlockSpec((TB, qw), lambda i: (i, 0)),
                  pl.BlockSpec((2, N_SUB_KEYS, N_SUB_KEYS), lambda i: (0, 0, 0))],
        out_specs=[out_blk, out_blk],
        out_shape=[jax.ShapeDtypeStruct((nb, HK, TB), jnp.int32), jax.ShapeDtypeStruct((nb, HK, TB), F32)],
        compiler_params=pltpu.CompilerParams(dimension_semantics=("parallel",), vmem_limit_bytes=VMEM_LIMIT),
        name="peer_topk",
    )(qp, sk)


def _pack_kernel(w_ref, o_ref):
    half = D_MODEL // 2
    words = pltpu.pack_elementwise([w_ref[:, :half], w_ref[:, half:]], packed_dtype=BF16)
    for s in range(ROWS_PER_EXPERT):
        o_ref[pl.ds(s, PACK_TILE, stride=ROWS_PER_EXPERT), :] = words[:, s * LANES:(s + 1) * LANES]


def _pack_table(w):
    return pl.pallas_call(
        _pack_kernel,
        grid=(N_EXPERTS // PACK_TILE,),
        in_specs=[pl.BlockSpec((PACK_TILE, D_MODEL), lambda i: (i, 0))],
        out_specs=pl.BlockSpec((PACK_TILE * ROWS_PER_EXPERT, LANES), lambda i: (i, 0)),
        out_shape=jax.ShapeDtypeStruct((N_EXPERTS * ROWS_PER_EXPERT, LANES), jnp.int32),
        compiler_params=pltpu.CompilerParams(dimension_semantics=("parallel",), vmem_limit_bytes=VMEM_LIMIT),
        name="pack_table",
    )(w)


def _unpack(w):
    return (pltpu.unpack_elementwise(w, index=0, packed_dtype=BF16, unpacked_dtype=F32),
            pltpu.unpack_elementwise(w, index=1, packed_dtype=BF16, unpacked_dtype=F32))


def _gather_rows(tab, idx_smem, stage, off, paired):
    sub = lax.broadcasted_iota(jnp.int32, (8, LANES), 0)
    for p in range(HK // 2):
        rows = [pl.multiple_of(idx_smem.at[pl.ds(e * TB, (HK + 1) * TB)][off], ROWS_PER_EXPERT)
                for e in (2 * p, 2 * p + 1)]
        if paired:
            even, odd = (tab[pl.ds(r, 8), :] for r in rows)
            stage[8 * p:8 * (p + 1), :] = jnp.where(sub < ROWS_PER_EXPERT, even, odd)
        else:
            stage[8 * p:8 * p + 4, :] = tab[pl.ds(rows[0], ROWS_PER_EXPERT), :]
            stage[8 * p + 4:8 * (p + 1), :] = tab[pl.ds(rows[1] + ROWS_PER_EXPERT, ROWS_PER_EXPERT), :]


def _start_block(i, n, idx_hbm, tab_hbm, tab, idx_smem, tab_sem, idx_sem):
    slot = lax.rem(i, 2)

    def idx_copy(blk, s):
        return pltpu.make_async_copy(idx_hbm.at[blk], idx_smem.at[pl.ds(s * (HK * TB), HK * TB)], idx_sem.at[s])

    @pl.when(i == 0)
    def _():
        idx_copy(0, 0).start()
        cp = pltpu.make_async_copy(tab_hbm, tab.at[pl.ds(TABLE_PAD, N_EXPERTS * ROWS_PER_EXPERT)], tab_sem)
        cp.start()
        pad = jnp.zeros((TABLE_PAD, LANES), jnp.int32)
        tab[0:TABLE_PAD, :] = pad
        tab[TABLE_PAD + N_EXPERTS * ROWS_PER_EXPERT:, :] = pad
        cp.wait()

    idx_copy(i, slot).wait()

    @pl.when(i + 1 < n)
    def _():
        idx_copy(i + 1, 1 - slot).start()

    return slot * (HK * TB)


def _expert_groups(stage):
    for j in range(HK // 8):
        yield j, [_unpack(stage[pl.ds(8 * ROWS_PER_EXPERT * j + s, 8, stride=ROWS_PER_EXPERT), :])
                  for s in range(ROWS_PER_EXPERT)]


def _peer_u_kernel(eidx_hbm, utab_hbm, h_ref, gates_ref, coef_ref, tab, idx_smem, *scratch):
    N_STAGE = U_STAGES
    stages = scratch[:N_STAGE]
    prods, a_scr, tab_sem, idx_sem = scratch[N_STAGE:]
    base = _start_block(pl.program_id(0), pl.num_programs(0), eidx_hbm, utab_hbm, tab, idx_smem, tab_sem, idx_sem)
    lane = lax.broadcasted_iota(jnp.int32, (HK, TB), 1)
    prods[0] = jnp.zeros((HK, LANES), F32)

    def products(stage, t):
        h = h_ref[t]
        hb = [jnp.broadcast_to(h[r:r + 1, :], (8, LANES)) for r in range(2 * ROWS_PER_EXPERT)]
        for j, rows in _expert_groups(stage):
            acc = None
            for s, (lo, hi) in enumerate(rows):
                term = lo * hb[s] + hi * hb[ROWS_PER_EXPERT + s]
                acc = term if acc is None else acc + term
            prods[t + 1, 8 * j:8 * (j + 1), :] = acc

    def reduce(t):
        col = jnp.sum(prods[t + 1], axis=1, keepdims=True)
        pltpu.store(a_scr, jnp.broadcast_to(col, (HK, TB)), mask=lane == t)

    for t in range(GATHER_AHEAD):
        _gather_rows(tab, idx_smem, stages[t], base + t, paired=True)

    def ring(k, carry):
        for u in range(N_STAGE):
            t = N_STAGE * k + u
            _gather_rows(tab, idx_smem, stages[(u + GATHER_AHEAD) % N_STAGE],
                         base + jnp.minimum(t + GATHER_AHEAD, TB - 1), paired=True)
            products(stages[u], t)
            reduce(t - 1)
        return carry

    lax.fori_loop(0, TB // N_STAGE, ring, 0)
    reduce(TB - 1)
    a = a_scr[...]
    gelu = 0.5 * a * (1.0 + lax.erf(a * (2.0 ** -0.5)))
    coef_ref[0] = gates_ref[0] * gelu


def _peer_v_kernel(eidx_hbm, vtab_hbm, coef_ref, x1_ref, o_ref, tab, idx_smem, *scratch):
    N_STAGE = V_STAGES
    stages = scratch[:N_STAGE]
    cbufs = scratch[N_STAGE:2 * N_STAGE]
    tab_sem, idx_sem = scratch[2 * N_STAGE:]
    base = _start_block(pl.program_id(0), pl.num_programs(0), eidx_hbm, vtab_hbm, tab, idx_smem, tab_sem, idx_sem)
    lane = lax.broadcasted_iota(jnp.int32, (HK, TB), 1)

    def fill(stage, cbuf, t):
        _gather_rows(tab, idx_smem, stage, base + t, paired=False)
        col = jnp.sum(jnp.where(lane == t, coef_ref[0], 0.0), axis=1, keepdims=True)
        cbuf[...] = jnp.broadcast_to(col, (HK, LANES))

    def weighted_sum(stage, cbuf, t):
        lo_acc = [None] * ROWS_PER_EXPERT
        hi_acc = [None] * ROWS_PER_EXPERT
        for j, rows in _expert_groups(stage):
            cb = cbuf[8 * j:8 * (j + 1), :]
            for s, (lo, hi) in enumerate(rows):
                lo_acc[s] = lo * cb if lo_acc[s] is None else lo_acc[s] + lo * cb
                hi_acc[s] = hi * cb if hi_acc[s] is None else hi_acc[s] + hi * cb
        rows = [jnp.sum(a, axis=0, keepdims=True) for a in lo_acc + hi_acc]
        o_ref[t] = x1_ref[t] + jnp.concatenate(rows, axis=0)

    for t in range(GATHER_AHEAD):
        fill(stages[t], cbufs[t], t)

    def ring(k, carry):
        for u in range(N_STAGE):
            t = N_STAGE * k + u
            nxt = (u + GATHER_AHEAD) % N_STAGE
            weighted_sum(stages[u], cbufs[u], t)
            fill(stages[nxt], cbufs[nxt], jnp.minimum(t + GATHER_AHEAD, TB - 1))
        return carry

    lax.fori_loop(0, TB // N_STAGE, ring, 0)


def _peer_gather(eidx, gates, h2, x1, utab, vtab):
    nb = TOKENS // TB
    eidx_flat = eidx.reshape(nb, HK * TB)
    any_spec = pl.BlockSpec(memory_space=pl.ANY)
    tok_blk = pl.BlockSpec((TB, 8, LANES), lambda i: (i, 0, 0))
    col_blk = pl.BlockSpec((1, HK, TB), lambda i: (i, 0, 0))
    table = pltpu.VMEM((N_EXPERTS * ROWS_PER_EXPERT + 2 * TABLE_PAD, LANES), jnp.int32)
    idx_scr = pltpu.SMEM((2 * HK * TB,), jnp.int32)
    stage = pltpu.VMEM((HK * ROWS_PER_EXPERT, LANES), jnp.int32)
    params = pltpu.CompilerParams(dimension_semantics=("arbitrary",), vmem_limit_bytes=VMEM_LIMIT)
    coef = pl.pallas_call(
        _peer_u_kernel,
        grid=(nb,),
        in_specs=[any_spec, any_spec, tok_blk, col_blk],
        out_specs=col_blk,
        out_shape=jax.ShapeDtypeStruct((nb, HK, TB), F32),
        scratch_shapes=[table, idx_scr] + [stage] * U_STAGES + [
            pltpu.VMEM((TB + 1, HK, LANES), F32), pltpu.VMEM((HK, TB), F32),
            pltpu.SemaphoreType.DMA, pltpu.SemaphoreType.DMA((2,))],
        compiler_params=params,
        name="peer_u",
    )(eidx_flat, utab, h2.reshape(TOKENS, 8, LANES), gates)
    return pl.pallas_call(
        _peer_v_kernel,
        grid=(nb,),
        in_specs=[any_spec, any_spec, col_blk, tok_blk],
        out_specs=tok_blk,
        out_shape=jax.ShapeDtypeStruct((TOKENS, 8, LANES), F32),
        scratch_shapes=[table, idx_scr] + [stage] * V_STAGES + [pltpu.VMEM((HK, LANES), F32)] * V_STAGES + [
            pltpu.SemaphoreType.DMA, pltpu.SemaphoreType.DMA((2,))],
        compiler_params=params,
        name="peer_v",
    )(eidx_flat, vtab, coef, x1.reshape(TOKENS, 8, LANES))


def kernel(x, norm1_g, w_in, q_norm_g, k_norm_g, sgu_norm_g, sgu_w, sgu_b, attn_out_g, sgu_out_g, w_out, norm2_g,
           w_query, sub_keys, expert_u, expert_v):
    b, s, _ = x.shape
    xf = x.reshape(TOKENS, D_MODEL)
    i = jnp.arange(1, ATTN_HEADS + 1, dtype=F32)
    slopes = jnp.exp2(-8.0 * i / ATTN_HEADS)
    g = lax.broadcasted_iota(jnp.int32, (LANES, LANES), 0) // HEAD_DIM
    bd = (g == g.T).astype(BF16)
    for layer in range(norm1_g.shape[0]):
        qn, kn, v, gn = _in_proj(
            xf, norm1_g[layer][None], w_in[layer].astype(BF16),
            jnp.tile(q_norm_g[layer], ATTN_HEADS)[None], jnp.tile(k_norm_g[layer], ATTN_HEADS)[None],
            sgu_norm_g[layer][None], bd, sgu_w[layer],
            jnp.repeat(sgu_b[layer].T, HEAD_DIM, axis=1), sgu_out_g[layer][None])
        attn = _attention(slopes, qn, kn, v)
        x1, h2, qp = _out_proj(attn, gn, xf, attn_out_g[layer][None], w_out[layer].astype(BF16),
                               norm2_g[layer][None], w_query[layer].astype(BF16))
        eidx, gates = _peer_topk(qp, sub_keys[layer].astype(BF16))
        xf = _peer_gather(eidx, gates, h2, x1, _pack_table(expert_u[layer]),
                          _pack_table(expert_v[layer])).reshape(TOKENS, D_MODEL)
    return xf.reshape(b, s, D_MODEL)
```

```python
import jax
import jax.numpy as jnp
from jax import lax
from jax.experimental import pallas as pl
from jax.experimental.pallas import tpu as pltpu

F32 = jnp.float32
BF16 = jnp.bfloat16

D_MODEL = 1024
BATCH = 8
SEQ = 4096
TOKENS = BATCH * SEQ
ATTN_HEADS = 8
HEAD_DIM = 64
ATTN_WIDTH = 512
SGU_GROUPS = 8
SGU_WIDTH = 512
SGU_CHUNK = 128
IN_WIDTH = 3 * ATTN_WIDTH + 2 * SGU_WIDTH
DILATED_BRANCHES = ((128, 1), (512, 4), (2048, 16))
ATTN_BLOCK = 128
ATTN_UNROLL = 8
PEER_HEADS = 8
N_SUB_KEYS = 128
N_EXPERTS = N_SUB_KEYS * N_SUB_KEYS
PEER_TOPK = 16
HK = PEER_HEADS * PEER_TOPK
U_PAIRED = True
U_STAGES = 2
V_STAGES = 8
GATHER_AHEAD = 1
EPS = 1e-6
NEG_INF = -1e30

LANES = 128
TM = 512
TB = 128
ROWS_PER_EXPERT = 4
TABLE_PAD = 8
PACK_TILE = 512
VMEM_LIMIT = 56 * 1024 * 1024


def _rms(x, gain):
    return x * lax.rsqrt(jnp.mean(x * x, axis=-1, keepdims=True) + EPS) * gain


def _group_rms(t, bd):
    sq = t * t
    hi = sq.astype(BF16)
    lo = (sq - hi.astype(F32)).astype(BF16)
    ssq = jnp.dot(hi, bd, preferred_element_type=F32) + jnp.dot(lo, bd, preferred_element_type=F32)
    return t * lax.rsqrt(ssq * (1.0 / HEAD_DIM) + EPS)


def _in_proj_kernel(x_ref, g1_ref, win_ref, qg_ref, kg_ref, sg_ref, bd_ref, wsgu_ref, bsgu_ref, sog_ref,
                    qn_ref, kn_ref, v_ref, gn_ref):
    h = _rms(x_ref[...], g1_ref[...])
    proj = jnp.dot(h.astype(BF16), win_ref[...], preferred_element_type=F32)
    bd = bd_ref[...]
    for c in range(ATTN_WIDTH // LANES):
        cs = slice(c * LANES, (c + 1) * LANES)
        qn_ref[:, cs] = _group_rms(proj[:, c * LANES:(c + 1) * LANES], bd) * qg_ref[:, cs]
        kn_ref[:, cs] = _group_rms(proj[:, ATTN_WIDTH + c * LANES:ATTN_WIDTH + (c + 1) * LANES], bd) * kg_ref[:, cs]
    v_ref[...] = proj[:, 2 * ATTN_WIDTH:3 * ATTN_WIDTH]

    su0 = 3 * ATTN_WIDTH
    sv0 = su0 + SGU_WIDTH
    row = lax.broadcasted_iota(jnp.int32, (SGU_CHUNK, SGU_CHUNK), 0)
    col = lax.broadcasted_iota(jnp.int32, (SGU_CHUNK, SGU_CHUNK), 1)
    causal = col <= row
    left = lax.broadcasted_iota(jnp.int32, (SGU_CHUNK, LANES), 1) < HEAD_DIM
    gated_cols = []
    for p in range(SGU_WIDTH // LANES):
        cs = slice(p * LANES, (p + 1) * LANES)
        svn = (_group_rms(proj[:, sv0 + p * LANES:sv0 + (p + 1) * LANES], bd) * sg_ref[:, cs]).astype(BF16)
        w0 = jnp.where(causal, wsgu_ref[2 * p], 0.0).astype(BF16)
        w1 = jnp.where(causal, wsgu_ref[2 * p + 1], 0.0).astype(BF16)
        chunks = []
        for c in range(TM // SGU_CHUNK):
            blk = svn[c * SGU_CHUNK:(c + 1) * SGU_CHUNK, :]
            r0 = jnp.dot(w0, blk, preferred_element_type=F32)
            r1 = jnp.dot(w1, blk, preferred_element_type=F32)
            chunks.append(jnp.where(left, r0, r1) + bsgu_ref[:, cs])
        mixed = jnp.concatenate(chunks, axis=0)
        gated_cols.append(proj[:, su0 + p * LANES:su0 + (p + 1) * LANES] * mixed)
    gated = jnp.concatenate(gated_cols, axis=1)
    gn_ref[...] = _rms(gated, sog_ref[...]).astype(BF16)


def _in_proj(x, g1, win, qg, kg, sg, bd, wsgu, bsgu, sog):
    full = lambda *shape: pl.BlockSpec(shape, lambda i: (0,) * len(shape))
    tile = lambda width: pl.BlockSpec((TM, width), lambda i: (i, 0))
    out = jax.ShapeDtypeStruct((TOKENS, ATTN_WIDTH), F32)
    return pl.pallas_call(
        _in_proj_kernel,
        grid=(TOKENS // TM,),
        in_specs=[tile(D_MODEL), full(1, D_MODEL), full(D_MODEL, IN_WIDTH), full(1, ATTN_WIDTH), full(1, ATTN_WIDTH),
                  full(1, SGU_WIDTH), full(LANES, LANES), full(SGU_GROUPS, SGU_CHUNK, SGU_CHUNK),
                  full(SGU_CHUNK, SGU_WIDTH), full(1, SGU_WIDTH)],
        out_specs=[tile(ATTN_WIDTH)] * 4,
        out_shape=[out] * 3 + [jax.ShapeDtypeStruct((TOKENS, SGU_WIDTH), BF16)],
        compiler_params=pltpu.CompilerParams(dimension_semantics=("parallel",), vmem_limit_bytes=VMEM_LIMIT),
        name="in_proj",
    )(x, g1, win, qg, kg, sg, bd, wsgu, bsgu, sog)


def _attn_kernel(slopes_ref, q_ref, k_ref, v_ref, o_ref, *scr):
    hp = pl.program_id(1)
    bias_scr = scr[-1]
    left = lax.broadcasted_iota(jnp.int32, (ATTN_BLOCK, LANES), 1) < HEAD_DIM
    left_kv = lax.broadcasted_iota(jnp.int32, (2 * ATTN_BLOCK, LANES), 1) < HEAD_DIM
    qi = lax.broadcasted_iota(jnp.int32, (ATTN_BLOCK, 2 * ATTN_BLOCK), 0)
    ki = lax.broadcasted_iota(jnp.int32, (ATTN_BLOCK, 2 * ATTN_BLOCK), 1)
    base = ATTN_BLOCK + qi - ki

    for br, (window, d) in enumerate(DILATED_BRANCHES):
        nblk = SEQ // d // ATTN_BLOCK
        w_sub = window // d
        o_scr, m_scr, mx_scr, lx_scr = scr[4 * br:4 * br + 4]

        for later in range(2):
            dist = base - ATTN_BLOCK * (1 - later)
            valid = (dist >= 0) & (dist <= w_sub)
            distf = (d * dist).astype(F32)
            for hh in range(2):
                bias_scr[hh, later] = jnp.where(valid, -slopes_ref[2 * hp + hh] * distf, NEG_INF)

        def rows(start, size):
            return pl.ds(start, size) if d == 1 else pl.ds(start, size, stride=d)

        def block(idx):
            r = idx // nblk
            n = idx % nblk
            later = jnp.minimum(n, 1)
            qs = d * ATTN_BLOCK * n + r
            ks = d * ATTN_BLOCK * (n - later) + r
            qb = q_ref[0, rows(qs, ATTN_BLOCK), :] * (HEAD_DIM ** -0.5)
            kb = k_ref[0, rows(ks, 2 * ATTN_BLOCK), :].astype(BF16)
            vb = v_ref[0, rows(ks, 2 * ATTN_BLOCK), :]
            res = []
            for hh in range(2):
                mine = left if hh == 0 else jnp.logical_not(left)
                mine_kv = left_kv if hh == 0 else jnp.logical_not(left_kv)
                qh = jnp.where(mine, qb, 0.0).astype(BF16)
                s = lax.dot_general(qh, kb, (((1,), (1,)), ((), ())), preferred_element_type=F32)
                s = s + bias_scr[hh, later]
                m = jnp.max(jnp.maximum(s[:, :ATTN_BLOCK], s[:, ATTN_BLOCK:]), axis=-1, keepdims=True)
                p = jnp.exp(s - m).astype(BF16)
                ol = jnp.dot(p, jnp.where(mine_kv, vb, 1.0).astype(BF16), preferred_element_type=F32)
                res.append((ol, m))
            dst = rows(qs, ATTN_BLOCK)
            o_scr[dst, :] = jnp.where(left, res[0][0], res[1][0])
            lx_scr[dst, :] = jnp.where(left, res[1][0], res[0][0])
            m_scr[dst, :] = jnp.where(left, res[0][1], res[1][1])
            mx_scr[dst, :] = jnp.where(left, res[1][1], res[0][1])

        def body(g, carry):
            for u in range(ATTN_UNROLL):
                block(g * ATTN_UNROLL + u)
            return carry

        lax.fori_loop(0, SEQ // ATTN_BLOCK // ATTN_UNROLL, body, 0)

    o_b, m_b, mx_b, lx_b = [[scr[4 * b + i][...] for b in range(3)] for i in range(4)]
    m_max = jnp.maximum(jnp.maximum(m_b[0], m_b[1]), m_b[2])
    num = sum(jnp.exp(m - m_max) * o for m, o in zip(m_b, o_b))
    mx_max = jnp.maximum(jnp.maximum(mx_b[0], mx_b[1]), mx_b[2])
    den_x = sum(jnp.exp(m - mx_max) * l for m, l in zip(mx_b, lx_b))
    o_ref[0] = num / pltpu.roll(den_x, HEAD_DIM, axis=1)


def _attention(slopes, qn, kn, v):
    blk = pl.BlockSpec((1, SEQ, LANES), lambda b, hp: (b, 0, hp))
    shape3 = (BATCH, SEQ, ATTN_WIDTH)
    return pl.pallas_call(
        _attn_kernel,
        grid=(BATCH, ATTN_WIDTH // LANES),
        in_specs=[pl.BlockSpec(memory_space=pltpu.SMEM), blk, blk, blk],
        out_specs=blk,
        out_shape=jax.ShapeDtypeStruct(shape3, F32),
        scratch_shapes=[pltpu.VMEM((SEQ, LANES), F32)] * 12 + [
            pltpu.VMEM((2, 2, ATTN_BLOCK, 2 * ATTN_BLOCK), F32)],
        compiler_params=pltpu.CompilerParams(dimension_semantics=("parallel", "parallel"),
                                             vmem_limit_bytes=VMEM_LIMIT),
        name="attention",
    )(slopes, qn.reshape(shape3), kn.reshape(shape3), v.reshape(shape3)).reshape(TOKENS, ATTN_WIDTH)


def _out_proj_kernel(attn_ref, gn_ref, x_ref, ag_ref, wout_ref, g2_ref, wq_ref, x1_ref, h2_ref, qp_ref):
    an = _rms(attn_ref[...], ag_ref[...])
    mixed = (jnp.dot(an.astype(BF16), wout_ref[:ATTN_WIDTH, :], preferred_element_type=F32)
             + jnp.dot(gn_ref[...].astype(BF16), wout_ref[ATTN_WIDTH:, :], preferred_element_type=F32))
    x1 = x_ref[...] + mixed
    h2 = _rms(x1, g2_ref[...])
    x1_ref[...] = x1
    h2_ref[...] = h2.reshape(TM, D_MODEL // LANES, LANES)
    qp_ref[...] = jnp.dot(h2.astype(BF16), wq_ref[...], preferred_element_type=F32).astype(BF16)


def _out_proj(attn, gn, x, ag, wout, g2, wq):
    full = lambda *shape: pl.BlockSpec(shape, lambda i: (0,) * len(shape))
    tile = lambda width: pl.BlockSpec((TM, width), lambda i: (i, 0))
    qw = wq.shape[1]
    rows = pl.BlockSpec((TM, D_MODEL // LANES, LANES), lambda i: (i, 0, 0))
    rows_shape = jax.ShapeDtypeStruct((TOKENS, D_MODEL // LANES, LANES), F32)
    return pl.pallas_call(
        _out_proj_kernel,
        grid=(TOKENS // TM,),
        in_specs=[tile(ATTN_WIDTH), tile(SGU_WIDTH), tile(D_MODEL), full(1, ATTN_WIDTH), full(D_MODEL, D_MODEL),
                  full(1, D_MODEL), full(D_MODEL, qw)],
        out_specs=[tile(D_MODEL), rows, tile(qw)],
        out_shape=[jax.ShapeDtypeStruct((TOKENS, D_MODEL), F32), rows_shape,
                   jax.ShapeDtypeStruct((TOKENS, qw), BF16)],
        compiler_params=pltpu.CompilerParams(dimension_semantics=("parallel",), vmem_limit_bytes=VMEM_LIMIT),
        name="out_proj",
    )(attn, gn, x, ag, wout, g2, wq)


def _top16(vals, ids, n_rows):
    pos = lax.broadcasted_iota(jnp.int32, (n_rows, TB), 0).astype(F32)
    best, picked = [], []
    for _ in range(PEER_TOPK):
        m = jnp.max(vals, axis=0, keepdims=True)
        p = jnp.min(jnp.where(vals == m, pos, float(n_rows)), axis=0, keepdims=True)
        sel = pos == p
        best.append(m)
        picked.append(p if ids is None else jnp.max(jnp.where(sel, ids, -1.0), axis=0, keepdims=True))
        vals = jnp.where(sel, -jnp.inf, vals)
    return jnp.concatenate(best, axis=0), jnp.concatenate(picked, axis=0)


def _candidates(v1, i1, v2, i2):
    sub = lax.broadcasted_iota(jnp.int32, (8, TB), 0)
    vals = [v1[0:1] + v2]
    ids = [i1[0:1] * N_SUB_KEYS + i2]
    for a in range(1, 8):
        live = PEER_TOPK // (a + 1)
        c = v1[a:a + 1] + v2[0:8]
        vals.append(c if live >= 8 else jnp.where(sub < live, c, -jnp.inf))
        ids.append(i1[a:a + 1] * N_SUB_KEYS + i2[0:8])
    vals.append(v1[8:16] + v2[0:1])
    ids.append(i1[8:16] * N_SUB_KEYS + i2[0:1])
    return jnp.concatenate(vals, axis=0), jnp.concatenate(ids, axis=0)


def _retrieve_head(h, qp_ref, sk_ref):
    tops = []
    for half in range(2):
        c0 = pl.multiple_of((2 * h + half) * N_SUB_KEYS, N_SUB_KEYS)
        qh = qp_ref[:, pl.ds(c0, N_SUB_KEYS)]
        s = lax.dot_general(sk_ref[half], qh, (((1,), (1,)), ((), ())), preferred_element_type=F32)
        tops.append(_top16(s, None, N_SUB_KEYS))
    (v1, i1), (v2, i2) = tops
    cand, cidx = _candidates(v1, i1, v2, i2)
    best, eid = _top16(cand, cidx, cand.shape[0])
    ex = jnp.exp(best - best[0:1, :])
    odd_slot = lax.broadcasted_iota(jnp.int32, (PEER_TOPK, TB), 0) % 2
    row = eid.astype(jnp.int32) * ROWS_PER_EXPERT + TABLE_PAD - ROWS_PER_EXPERT * odd_slot
    return row, ex / jnp.sum(ex, axis=0, keepdims=True)


def _topk_kernel(qp_ref, sk_ref, eidx_ref, gates_ref):
    for h in range(PEER_HEADS):
        rs = slice(h * PEER_TOPK, (h + 1) * PEER_TOPK)
        eidx_ref[0, rs, :], gates_ref[0, rs, :] = _retrieve_head(h, qp_ref, sk_ref)


def _peer_topk(qp, sk):
    nb = TOKENS // TB
    qw = qp.shape[1]
    out_blk = pl.BlockSpec((1, HK, TB), lambda i: (i, 0, 0))
    return pl.pallas_call(
        _topk_kernel,
        grid=(nb,),
        in_specs=[pl.BlockSpec((TB, qw), lambda i: (i, 0)),
                  pl.BlockSpec((2, N_SUB_KEYS, N_SUB_KEYS), lambda i: (0, 0, 0))],
        out_specs=[out_blk, out_blk],
        out_shape=[jax.ShapeDtypeStruct((nb, HK, TB), jnp.int32), jax.ShapeDtypeStruct((nb, HK, TB), F32)],
        compiler_params=pltpu.CompilerParams(dimension_semantics=("parallel",), vmem_limit_bytes=VMEM_LIMIT),
        name="peer_topk",
    )(qp, sk)


def _pack_kernel(w_ref, o_ref):
    half = D_MODEL // 2
    words = pltpu.pack_elementwise([w_ref[:, :half], w_ref[:, half:]], packed_dtype=BF16)
    for s in range(ROWS_PER_EXPERT):
        o_ref[pl.ds(s, PACK_TILE, stride=ROWS_PER_EXPERT), :] = words[:, s * LANES:(s + 1) * LANES]


def _pack_table(w):
    return pl.pallas_call(
        _pack_kernel,
        grid=(N_EXPERTS // PACK_TILE,),
        in_specs=[pl.BlockSpec((PACK_TILE, D_MODEL), lambda i: (i, 0))],
        out_specs=pl.BlockSpec((PACK_TILE * ROWS_PER_EXPERT, LANES), lambda i: (i, 0)),
        out_shape=jax.ShapeDtypeStruct((N_EXPERTS * ROWS_PER_EXPERT, LANES), jnp.int32),
        compiler_params=pltpu.CompilerParams(dimension_semantics=("parallel",), vmem_limit_bytes=VMEM_LIMIT),
        name="pack_table",
    )(w)


def _unpack(w):
    return (pltpu.unpack_elementwise(w, index=0, packed_dtype=BF16, unpacked_dtype=F32),
            pltpu.unpack_elementwise(w, index=1, packed_dtype=BF16, unpacked_dtype=F32))


def _gather_rows(tab, idx_smem, stage, off, paired):
    sub = lax.broadcasted_iota(jnp.int32, (8, LANES), 0)
    for p in range(HK // 2):
        rows = [pl.multiple_of(idx_smem.at[pl.ds(e * TB, (HK + 1) * TB)][off], ROWS_PER_EXPERT)
                for e in (2 * p, 2 * p + 1)]
        if paired:
            even, odd = (tab[pl.ds(r, 8), :] for r in rows)
            stage[8 * p:8 * (p + 1), :] = jnp.where(sub < ROWS_PER_EXPERT, even, odd)
        else:
            stage[8 * p:8 * p + 4, :] = tab[pl.ds(rows[0], ROWS_PER_EXPERT), :]
            stage[8 * p + 4:8 * (p + 1), :] = tab[pl.ds(rows[1] + ROWS_PER_EXPERT, ROWS_PER_EXPERT), :]


def _start_block(i, n, idx_hbm, tab_hbm, tab, idx_smem, tab_sem, idx_sem):
    slot = lax.rem(i, 2)

    def idx_copy(blk, s):
        return pltpu.make_async_copy(idx_hbm.at[blk], idx_smem.at[pl.ds(s * (HK * TB), HK * TB)], idx_sem.at[s])

    @pl.when(i == 0)
    def _():
        idx_copy(0, 0).start()
        cp = pltpu.make_async_copy(tab_hbm, tab.at[pl.ds(TABLE_PAD, N_EXPERTS * ROWS_PER_EXPERT)], tab_sem)
        cp.start()
        pad = jnp.zeros((TABLE_PAD, LANES), jnp.int32)
        tab[0:TABLE_PAD, :] = pad
        tab[TABLE_PAD + N_EXPERTS * ROWS_PER_EXPERT:, :] = pad
        cp.wait()

    idx_copy(i, slot).wait()

    @pl.when(i + 1 < n)
    def _():
        idx_copy(i + 1, 1 - slot).start()

    return slot * (HK * TB)


def _expert_groups(stage):
    for j in range(HK // 8):
        yield j, [_unpack(stage[pl.ds(8 * ROWS_PER_EXPERT * j + s, 8, stride=ROWS_PER_EXPERT), :])
                  for s in range(ROWS_PER_EXPERT)]


def _peer_u_kernel(eidx_hbm, utab_hbm, h_ref, gates_ref, coef_ref, tab, idx_smem, *scratch):
    N_STAGE = U_STAGES
    stages = scratch[:N_STAGE]
    prods, a_scr, tab_sem, idx_sem = scratch[N_STAGE:]
    base = _start_block(pl.program_id(0), pl.num_programs(0), eidx_hbm, utab_hbm, tab, idx_smem, tab_sem, idx_sem)
    lane = lax.broadcasted_iota(jnp.int32, (HK, TB), 1)
    prods[0] = jnp.zeros((HK, LANES), F32)

    def products(stage, t):
        h = h_ref[t]
        hb = [jnp.broadcast_to(h[r:r + 1, :], (8, LANES)) for r in range(2 * ROWS_PER_EXPERT)]
        for j, rows in _expert_groups(stage):
            acc = None
            for s, (lo, hi) in enumerate(rows):
                term = lo * hb[s] + hi * hb[ROWS_PER_EXPERT + s]
                acc = term if acc is None else acc + term
            prods[t + 1, 8 * j:8 * (j + 1), :] = acc

    def reduce(t):
        col = jnp.sum(prods[t + 1], axis=1, keepdims=True)
        pltpu.store(a_scr, jnp.broadcast_to(col, (HK, TB)), mask=lane == t)

    for t in range(GATHER_AHEAD):
        _gather_rows(tab, idx_smem, stages[t], base + t, paired=U_PAIRED)

    def ring(k, carry):
        for u in range(N_STAGE):
            t = N_STAGE * k + u
            _gather_rows(tab, idx_smem, stages[(u + GATHER_AHEAD) % N_STAGE],
                         base + jnp.minimum(t + GATHER_AHEAD, TB - 1), paired=U_PAIRED)
            products(stages[u], t)
            reduce(t - 1)
        return carry

    lax.fori_loop(0, TB // N_STAGE, ring, 0)
    reduce(TB - 1)
    a = a_scr[...]
    gelu = 0.5 * a * (1.0 + lax.erf(a * (2.0 ** -0.5)))
    coef_ref[0] = gates_ref[0] * gelu


def _peer_v_kernel(eidx_hbm, vtab_hbm, coef_ref, x1_ref, o_ref, tab, idx_smem, *scratch):
    N_STAGE = V_STAGES
    stages = scratch[:N_STAGE]
    cbufs = scratch[N_STAGE:2 * N_STAGE]
    out_tiles, tab_sem, idx_sem = scratch[2 * N_STAGE:]
    base = _start_block(pl.program_id(0), pl.num_programs(0), eidx_hbm, vtab_hbm, tab, idx_smem, tab_sem, idx_sem)
    lane = lax.broadcasted_iota(jnp.int32, (HK, TB), 1)
    out_tiles[...] = x1_ref[...].reshape(TB, D_MODEL // LANES, LANES)

    def fill(stage, cbuf, t):
        _gather_rows(tab, idx_smem, stage, base + t, paired=False)
        col = jnp.sum(jnp.where(lane == t, coef_ref[0], 0.0), axis=1, keepdims=True)
        cbuf[...] = jnp.broadcast_to(col, (HK, LANES))

    def weighted_sum(stage, cbuf, t):
        lo_acc = [None] * ROWS_PER_EXPERT
        hi_acc = [None] * ROWS_PER_EXPERT
        for j, rows in _expert_groups(stage):
            cb = cbuf[8 * j:8 * (j + 1), :]
            for s, (lo, hi) in enumerate(rows):
                lo_acc[s] = lo * cb if lo_acc[s] is None else lo_acc[s] + lo * cb
                hi_acc[s] = hi * cb if hi_acc[s] is None else hi_acc[s] + hi * cb
        rows = [jnp.sum(a, axis=0, keepdims=True) for a in lo_acc + hi_acc]
        out_tiles[t] = out_tiles[t] + jnp.concatenate(rows, axis=0)

    for t in range(GATHER_AHEAD):
        fill(stages[t], cbufs[t], t)

    def ring(k, carry):
        for u in range(N_STAGE):
            t = N_STAGE * k + u
            nxt = (u + GATHER_AHEAD) % N_STAGE
            weighted_sum(stages[u], cbufs[u], t)
            fill(stages[nxt], cbufs[nxt], jnp.minimum(t + GATHER_AHEAD, TB - 1))
        return carry

    lax.fori_loop(0, TB // N_STAGE, ring, 0)
    o_ref[...] = out_tiles[...].reshape(TB, D_MODEL)


def _peer_gather(eidx, gates, h2, x1, utab, vtab):
    nb = TOKENS // TB
    eidx_flat = eidx.reshape(nb, HK * TB)
    any_spec = pl.BlockSpec(memory_space=pl.ANY)
    tok_blk = pl.BlockSpec((TB, 8, LANES), lambda i: (i, 0, 0))
    row_blk = pl.BlockSpec((TB, D_MODEL), lambda i: (i, 0))
    col_blk = pl.BlockSpec((1, HK, TB), lambda i: (i, 0, 0))
    table = pltpu.VMEM((N_EXPERTS * ROWS_PER_EXPERT + 2 * TABLE_PAD, LANES), jnp.int32)
    idx_scr = pltpu.SMEM((2 * HK * TB,), jnp.int32)
    stage = pltpu.VMEM((HK * ROWS_PER_EXPERT, LANES), jnp.int32)
    params = pltpu.CompilerParams(dimension_semantics=("arbitrary",), vmem_limit_bytes=VMEM_LIMIT)
    coef = pl.pallas_call(
        _peer_u_kernel,
        grid=(nb,),
        in_specs=[any_spec, any_spec, tok_blk, col_blk],
        out_specs=col_blk,
        out_shape=jax.ShapeDtypeStruct((nb, HK, TB), F32),
        scratch_shapes=[table, idx_scr] + [stage] * U_STAGES + [
            pltpu.VMEM((TB + 1, HK, LANES), F32), pltpu.VMEM((HK, TB), F32),
            pltpu.SemaphoreType.DMA, pltpu.SemaphoreType.DMA((2,))],
        compiler_params=params,
        name="peer_u",
    )(eidx_flat, utab, h2.reshape(TOKENS, 8, LANES), gates)
    return pl.pallas_call(
        _peer_v_kernel,
        grid=(nb,),
        in_specs=[any_spec, any_spec, col_blk, row_blk],
        out_specs=row_blk,
        out_shape=jax.ShapeDtypeStruct((TOKENS, D_MODEL), F32),
        scratch_shapes=[table, idx_scr] + [stage] * V_STAGES + [pltpu.VMEM((HK, LANES), F32)] * V_STAGES + [
            pltpu.VMEM((TB, D_MODEL // LANES, LANES), F32), pltpu.SemaphoreType.DMA, pltpu.SemaphoreType.DMA((2,))],
        compiler_params=params,
        name="peer_v",
    )(eidx_flat, vtab, coef, x1)


def kernel(x, norm1_g, w_in, q_norm_g, k_norm_g, sgu_norm_g, sgu_w, sgu_b, attn_out_g, sgu_out_g, w_out, norm2_g,
           w_query, sub_keys, expert_u, expert_v):
    b, s, _ = x.shape
    xf = x.reshape(TOKENS, D_MODEL)
    i = jnp.arange(1, ATTN_HEADS + 1, dtype=F32)
    slopes = jnp.exp2(-8.0 * i / ATTN_HEADS)
    g = lax.broadcasted_iota(jnp.int32, (LANES, LANES), 0) // HEAD_DIM
    bd = (g == g.T).astype(BF16)
    for layer in range(norm1_g.shape[0]):
        qn, kn, v, gn = _in_proj(
            xf, norm1_g[layer][None], w_in[layer].astype(BF16),
            jnp.tile(q_norm_g[layer], ATTN_HEADS)[None], jnp.tile(k_norm_g[layer], ATTN_HEADS)[None],
            sgu_norm_g[layer][None], bd, sgu_w[layer],
            jnp.repeat(sgu_b[layer].T, HEAD_DIM, axis=1), sgu_out_g[layer][None])
        attn = _attention(slopes, qn, kn, v)
        x1, h2, qp = _out_proj(attn, gn, xf, attn_out_g[layer][None], w_out[layer].astype(BF16),
                               norm2_g[layer][None], w_query[layer].astype(BF16))
        eidx, gates = _peer_topk(qp, sub_keys[layer].astype(BF16))
        xf = _peer_gather(eidx, gates, h2, x1, _pack_table(expert_u[layer]),
                          _pack_table(expert_v[layer])).reshape(TOKENS, D_MODEL)
    return xf.reshape(b, s, D_MODEL)
```

```python
import jax
import jax.numpy as jnp
from jax import lax
from jax.experimental import pallas as pl
from jax.experimental.pallas import tpu as pltpu

F32 = jnp.float32
BF16 = jnp.bfloat16

D_MODEL = 1024
BATCH = 8
SEQ = 4096
TOKENS = BATCH * SEQ
ATTN_HEADS = 8
HEAD_DIM = 64
ATTN_WIDTH = 512
SGU_GROUPS = 8
SGU_WIDTH = 512
SGU_CHUNK = 128
IN_WIDTH = 3 * ATTN_WIDTH + 2 * SGU_WIDTH
DILATED_BRANCHES = ((128, 1), (512, 4), (2048, 16))
ATTN_BLOCK = 128
ATTN_UNROLL = 8
PEER_HEADS = 8
N_SUB_KEYS = 128
N_EXPERTS = N_SUB_KEYS * N_SUB_KEYS
PEER_TOPK = 16
HK = PEER_HEADS * PEER_TOPK
U_PAIRED = True
U_STAGES = 2
V_STAGES = 8
GATHER_AHEAD = 1
EPS = 1e-6
NEG_INF = -1e30

LANES = 128
TM = 512
TB = 128
ROWS_PER_EXPERT = 4
TABLE_PAD = 8
PACK_TILE = 512
VMEM_LIMIT = 56 * 1024 * 1024


def _rms(x, gain):
    return x * lax.rsqrt(jnp.mean(x * x, axis=-1, keepdims=True) + EPS) * gain


def _group_rms(t, bd):
    sq = t * t
    hi = sq.astype(BF16)
    lo = (sq - hi.astype(F32)).astype(BF16)
    ssq = jnp.dot(hi, bd, preferred_element_type=F32) + jnp.dot(lo, bd, preferred_element_type=F32)
    return t * lax.rsqrt(ssq * (1.0 / HEAD_DIM) + EPS)


def _in_proj_kernel(x_ref, g1_ref, win_ref, qg_ref, kg_ref, sg_ref, bd_ref, wsgu_ref, bsgu_ref, sog_ref,
                    qn_ref, kn_ref, v_ref, gn_ref):
    h = _rms(x_ref[...], g1_ref[...])
    proj = jnp.dot(h.astype(BF16), win_ref[...], preferred_element_type=F32)
    bd = bd_ref[...]
    for c in range(ATTN_WIDTH // LANES):
        cs = slice(c * LANES, (c + 1) * LANES)
        qn_ref[:, cs] = _group_rms(proj[:, c * LANES:(c + 1) * LANES], bd) * qg_ref[:, cs]
        kn_ref[:, cs] = _group_rms(proj[:, ATTN_WIDTH + c * LANES:ATTN_WIDTH + (c + 1) * LANES], bd) * kg_ref[:, cs]
    v_ref[...] = proj[:, 2 * ATTN_WIDTH:3 * ATTN_WIDTH]

    su0 = 3 * ATTN_WIDTH
    sv0 = su0 + SGU_WIDTH
    row = lax.broadcasted_iota(jnp.int32, (SGU_CHUNK, SGU_CHUNK), 0)
    col = lax.broadcasted_iota(jnp.int32, (SGU_CHUNK, SGU_CHUNK), 1)
    causal = col <= row
    left = lax.broadcasted_iota(jnp.int32, (SGU_CHUNK, LANES), 1) < HEAD_DIM
    gated_cols = []
    for p in range(SGU_WIDTH // LANES):
        cs = slice(p * LANES, (p + 1) * LANES)
        svn = (_group_rms(proj[:, sv0 + p * LANES:sv0 + (p + 1) * LANES], bd) * sg_ref[:, cs]).astype(BF16)
        w0 = jnp.where(causal, wsgu_ref[2 * p], 0.0).astype(BF16)
        w1 = jnp.where(causal, wsgu_ref[2 * p + 1], 0.0).astype(BF16)
        chunks = []
        for c in range(TM // SGU_CHUNK):
            blk = svn[c * SGU_CHUNK:(c + 1) * SGU_CHUNK, :]
            r0 = jnp.dot(w0, blk, preferred_element_type=F32)
            r1 = jnp.dot(w1, blk, preferred_element_type=F32)
            chunks.append(jnp.where(left, r0, r1) + bsgu_ref[:, cs])
        mixed = jnp.concatenate(chunks, axis=0)
        gated_cols.append(proj[:, su0 + p * LANES:su0 + (p + 1) * LANES] * mixed)
    gated = jnp.concatenate(gated_cols, axis=1)
    gn_ref[...] = _rms(gated, sog_ref[...]).astype(BF16)


def _in_proj(x, g1, win, qg, kg, sg, bd, wsgu, bsgu, sog):
    full = lambda *shape: pl.BlockSpec(shape, lambda i: (0,) * len(shape))
    tile = lambda width: pl.BlockSpec((TM, width), lambda i: (i, 0))
    out = jax.ShapeDtypeStruct((TOKENS, ATTN_WIDTH), F32)
    return pl.pallas_call(
        _in_proj_kernel,
        grid=(TOKENS // TM,),
        in_specs=[tile(D_MODEL), full(1, D_MODEL), full(D_MODEL, IN_WIDTH), full(1, ATTN_WIDTH), full(1, ATTN_WIDTH),
                  full(1, SGU_WIDTH), full(LANES, LANES), full(SGU_GROUPS, SGU_CHUNK, SGU_CHUNK),
                  full(SGU_CHUNK, SGU_WIDTH), full(1, SGU_WIDTH)],
        out_specs=[tile(ATTN_WIDTH)] * 4,
        out_shape=[out] * 3 + [jax.ShapeDtypeStruct((TOKENS, SGU_WIDTH), BF16)],
        compiler_params=pltpu.CompilerParams(dimension_semantics=("parallel",), vmem_limit_bytes=VMEM_LIMIT),
        name="in_proj",
    )(x, g1, win, qg, kg, sg, bd, wsgu, bsgu, sog)


def _attn_kernel(slopes_ref, q_ref, k_ref, v_ref, o_ref, *scr):
    hp = pl.program_id(1)
    bias_scr = scr[-1]
    left = lax.broadcasted_iota(jnp.int32, (ATTN_BLOCK, LANES), 1) < HEAD_DIM
    left_kv = lax.broadcasted_iota(jnp.int32, (2 * ATTN_BLOCK, LANES), 1) < HEAD_DIM
    qi = lax.broadcasted_iota(jnp.int32, (ATTN_BLOCK, 2 * ATTN_BLOCK), 0)
    ki = lax.broadcasted_iota(jnp.int32, (ATTN_BLOCK, 2 * ATTN_BLOCK), 1)
    base = ATTN_BLOCK + qi - ki

    for br, (window, d) in enumerate(DILATED_BRANCHES):
        nblk = SEQ // d // ATTN_BLOCK
        w_sub = window // d
        o_scr, m_scr, mx_scr, lx_scr = scr[4 * br:4 * br + 4]

        for later in range(2):
            dist = base - ATTN_BLOCK * (1 - later)
            valid = (dist >= 0) & (dist <= w_sub)
            distf = (d * dist).astype(F32)
            for hh in range(2):
                bias_scr[hh, later] = jnp.where(valid, -slopes_ref[2 * hp + hh] * distf, NEG_INF)

        def rows(start, size):
            return pl.ds(start, size) if d == 1 else pl.ds(start, size, stride=d)

        def block(idx):
            r = idx // nblk
            n = idx % nblk
            later = jnp.minimum(n, 1)
            qs = d * ATTN_BLOCK * n + r
            ks = d * ATTN_BLOCK * (n - later) + r
            qb = q_ref[0, rows(qs, ATTN_BLOCK), :] * (HEAD_DIM ** -0.5)
            kb = k_ref[0, rows(ks, 2 * ATTN_BLOCK), :].astype(BF16)
            vb = v_ref[0, rows(ks, 2 * ATTN_BLOCK), :]
            res = []
            for hh in range(2):
                mine = left if hh == 0 else jnp.logical_not(left)
                mine_kv = left_kv if hh == 0 else jnp.logical_not(left_kv)
                qh = jnp.where(mine, qb, 0.0).astype(BF16)
                s = lax.dot_general(qh, kb, (((1,), (1,)), ((), ())), preferred_element_type=F32)
                s = s + bias_scr[hh, later]
                m = jnp.max(jnp.maximum(s[:, :ATTN_BLOCK], s[:, ATTN_BLOCK:]), axis=-1, keepdims=True)
                p = jnp.exp(s - m).astype(BF16)
                ol = jnp.dot(p, jnp.where(mine_kv, vb, 1.0).astype(BF16), preferred_element_type=F32)
                res.append((ol, m))
            dst = rows(qs, ATTN_BLOCK)
            o_scr[dst, :] = jnp.where(left, res[0][0], res[1][0])
            lx_scr[dst, :] = jnp.where(left, res[1][0], res[0][0])
            m_scr[dst, :] = jnp.where(left, res[0][1], res[1][1])
            mx_scr[dst, :] = jnp.where(left, res[1][1], res[0][1])

        def body(g, carry):
            for u in range(ATTN_UNROLL):
                block(g * ATTN_UNROLL + u)
            return carry

        lax.fori_loop(0, SEQ // ATTN_BLOCK // ATTN_UNROLL, body, 0)

    o_b, m_b, mx_b, lx_b = [[scr[4 * b + i][...] for b in range(3)] for i in range(4)]
    m_max = jnp.maximum(jnp.maximum(m_b[0], m_b[1]), m_b[2])
    num = sum(jnp.exp(m - m_max) * o for m, o in zip(m_b, o_b))
    mx_max = jnp.maximum(jnp.maximum(mx_b[0], mx_b[1]), mx_b[2])
    den_x = sum(jnp.exp(m - mx_max) * l for m, l in zip(mx_b, lx_b))
    o_ref[0] = num / pltpu.roll(den_x, HEAD_DIM, axis=1)


def _attention(slopes, qn, kn, v):
    blk = pl.BlockSpec((1, SEQ, LANES), lambda b, hp: (b, 0, hp))
    shape3 = (BATCH, SEQ, ATTN_WIDTH)
    return pl.pallas_call(
        _attn_kernel,
        grid=(BATCH, ATTN_WIDTH // LANES),
        in_specs=[pl.BlockSpec(memory_space=pltpu.SMEM), blk, blk, blk],
        out_specs=blk,
        out_shape=jax.ShapeDtypeStruct(shape3, F32),
        scratch_shapes=[pltpu.VMEM((SEQ, LANES), F32)] * 12 + [
            pltpu.VMEM((2, 2, ATTN_BLOCK, 2 * ATTN_BLOCK), F32)],
        compiler_params=pltpu.CompilerParams(dimension_semantics=("parallel", "parallel"),
                                             vmem_limit_bytes=VMEM_LIMIT),
        name="attention",
    )(slopes, qn.reshape(shape3), kn.reshape(shape3), v.reshape(shape3)).reshape(TOKENS, ATTN_WIDTH)


def _out_proj_kernel(attn_ref, gn_ref, x_ref, ag_ref, wout_ref, g2_ref, wq_ref, x1_ref, h2_ref, qp_ref):
    an = _rms(attn_ref[...], ag_ref[...])
    mixed = (jnp.dot(an.astype(BF16), wout_ref[:ATTN_WIDTH, :], preferred_element_type=F32)
             + jnp.dot(gn_ref[...].astype(BF16), wout_ref[ATTN_WIDTH:, :], preferred_element_type=F32))
    x1 = x_ref[...] + mixed
    h2 = _rms(x1, g2_ref[...])
    x1_ref[...] = x1
    h2_ref[...] = h2.reshape(TM, D_MODEL // LANES, LANES)
    qp_ref[...] = jnp.dot(h2.astype(BF16), wq_ref[...], preferred_element_type=F32).astype(BF16)


def _out_proj(attn, gn, x, ag, wout, g2, wq):
    full = lambda *shape: pl.BlockSpec(shape, lambda i: (0,) * len(shape))
    tile = lambda width: pl.BlockSpec((TM, width), lambda i: (i, 0))
    qw = wq.shape[1]
    rows = pl.BlockSpec((TM, D_MODEL // LANES, LANES), lambda i: (i, 0, 0))
    rows_shape = jax.ShapeDtypeStruct((TOKENS, D_MODEL // LANES, LANES), F32)
    return pl.pallas_call(
        _out_proj_kernel,
        grid=(TOKENS // TM,),
        in_specs=[tile(ATTN_WIDTH), tile(SGU_WIDTH), tile(D_MODEL), full(1, ATTN_WIDTH), full(D_MODEL, D_MODEL),
                  full(1, D_MODEL), full(D_MODEL, qw)],
        out_specs=[tile(D_MODEL), rows, tile(qw)],
        out_shape=[jax.ShapeDtypeStruct((TOKENS, D_MODEL), F32), rows_shape,
                   jax.ShapeDtypeStruct((TOKENS, qw), BF16)],
        compiler_params=pltpu.CompilerParams(dimension_semantics=("parallel",), vmem_limit_bytes=VMEM_LIMIT),
        name="out_proj",
    )(attn, gn, x, ag, wout, g2, wq)


def _sort_network(n):
    pairs, p = [], 1
    while p < n:
        k = p
        while k >= 1:
            for j in range(k % p, n - k, 2 * k):
                for i in range(min(k, n - j - k)):
                    if (i + j) // (2 * p) == (i + j + k) // (2 * p):
                        pairs.append((i + j, i + j + k))
            k //= 2
        p *= 2
    return pairs


def _sublane_allreduce(x, op):
    for shift in (4, 2, 1):
        x = op(x, pltpu.roll(x, shift, 0))
    return x


def _top16(vals, ids, n_rows):
    levels = n_rows // 8
    sub = lax.broadcasted_iota(jnp.int32, (8, TB), 0).astype(F32)
    v = [vals[8 * i:8 * (i + 1)] for i in range(levels)]
    p = [sub + 8.0 * i for i in range(levels)]
    payload = [v, p] if ids is None else [v, p, [ids[8 * i:8 * (i + 1)] for i in range(levels)]]
    for i, j in _sort_network(16):
        if j < levels:
            swap = (v[j] > v[i]) | ((v[j] == v[i]) & (p[j] < p[i]))
            for a in payload:
                a[i], a[j] = jnp.where(swap, a[j], a[i]), jnp.where(swap, a[i], a[j])
    best, picked = [], []
    for k in range(PEER_TOPK):
        m = _sublane_allreduce(v[0], jnp.maximum)
        row = _sublane_allreduce(jnp.where(v[0] == m, p[0], float(n_rows)), jnp.minimum)
        hit = p[0] == row
        best.append(m[0:1])
        picked.append(row[0:1] if ids is None else
                      _sublane_allreduce(jnp.where(hit, payload[2][0], -1.0), jnp.maximum)[0:1])
        last = min(levels, PEER_TOPK - k) - 1
        for a in payload:
            for i in range(last):
                a[i] = jnp.where(hit, a[i + 1], a[i])
        v[last] = jnp.where(hit, -jnp.inf, v[last])
    return jnp.concatenate(best, axis=0), jnp.concatenate(picked, axis=0)


def _candidates(v1, i1, v2, i2):
    sub = lax.broadcasted_iota(jnp.int32, (8, TB), 0)
    vals = [v1[0:1] + v2]
    ids = [i1[0:1] * N_SUB_KEYS + i2]
    for a in range(1, 8):
        live = PEER_TOPK // (a + 1)
        c = v1[a:a + 1] + v2[0:8]
        vals.append(c if live >= 8 else jnp.where(sub < live, c, -jnp.inf))
        ids.append(i1[a:a + 1] * N_SUB_KEYS + i2[0:8])
    vals.append(v1[8:16] + v2[0:1])
    ids.append(i1[8:16] * N_SUB_KEYS + i2[0:1])
    return jnp.concatenate(vals, axis=0), jnp.concatenate(ids, axis=0)


def _retrieve_head(h, qp_ref, sk_ref):
    tops = []
    for half in range(2):
        c0 = pl.multiple_of((2 * h + half) * N_SUB_KEYS, N_SUB_KEYS)
        qh = qp_ref[:, pl.ds(c0, N_SUB_KEYS)]
        s = lax.dot_general(sk_ref[half], qh, (((1,), (1,)), ((), ())), preferred_element_type=F32)
        tops.append(_top16(s, None, N_SUB_KEYS))
    (v1, i1), (v2, i2) = tops
    cand, cidx = _candidates(v1, i1, v2, i2)
    best, eid = _top16(cand, cidx, cand.shape[0])
    ex = jnp.exp(best - best[0:1, :])
    odd_slot = lax.broadcasted_iota(jnp.int32, (PEER_TOPK, TB), 0) % 2
    row = eid.astype(jnp.int32) * ROWS_PER_EXPERT + TABLE_PAD - ROWS_PER_EXPERT * odd_slot
    return row, ex / jnp.sum(ex, axis=0, keepdims=True)


def _topk_kernel(qp_ref, sk_ref, eidx_ref, gates_ref):
    for h in range(PEER_HEADS):
        rs = slice(h * PEER_TOPK, (h + 1) * PEER_TOPK)
        eidx_ref[0, rs, :], gates_ref[0, rs, :] = _retrieve_head(h, qp_ref, sk_ref)


def _peer_topk(qp, sk):
    nb = TOKENS // TB
    qw = qp.shape[1]
    out_blk = pl.BlockSpec((1, HK, TB), lambda i: (i, 0, 0))
    return pl.pallas_call(
        _topk_kernel,
        grid=(nb,),
        in_specs=[pl.BlockSpec((TB, qw), lambda i: (i, 0)),
                  pl.BlockSpec((2, N_SUB_KEYS, N_SUB_KEYS), lambda i: (0, 0, 0))],
        out_specs=[out_blk, out_blk],
        out_shape=[jax.ShapeDtypeStruct((nb, HK, TB), jnp.int32), jax.ShapeDtypeStruct((nb, HK, TB), F32)],
        compiler_params=pltpu.CompilerParams(dimension_semantics=("parallel",), vmem_limit_bytes=VMEM_LIMIT),
        name="peer_topk",
    )(qp, sk)


def _pack_kernel(w_ref, o_ref):
    half = D_MODEL // 2
    words = pltpu.pack_elementwise([w_ref[:, :half], w_ref[:, half:]], packed_dtype=BF16)
    for s in range(ROWS_PER_EXPERT):
        o_ref[pl.ds(s, PACK_TILE, stride=ROWS_PER_EXPERT), :] = words[:, s * LANES:(s + 1) * LANES]


def _pack_table(w):
    return pl.pallas_call(
        _pack_kernel,
        grid=(N_EXPERTS // PACK_TILE,),
        in_specs=[pl.BlockSpec((PACK_TILE, D_MODEL), lambda i: (i, 0))],
        out_specs=pl.BlockSpec((PACK_TILE * ROWS_PER_EXPERT, LANES), lambda i: (i, 0)),
        out_shape=jax.ShapeDtypeStruct((N_EXPERTS * ROWS_PER_EXPERT, LANES), jnp.int32),
        compiler_params=pltpu.CompilerParams(dimension_semantics=("parallel",), vmem_limit_bytes=VMEM_LIMIT),
        name="pack_table",
    )(w)


def _unpack(w):
    return (pltpu.unpack_elementwise(w, index=0, packed_dtype=BF16, unpacked_dtype=F32),
            pltpu.unpack_elementwise(w, index=1, packed_dtype=BF16, unpacked_dtype=F32))


def _gather_rows(tab, idx_smem, stage, off, paired):
    sub = lax.broadcasted_iota(jnp.int32, (8, LANES), 0)
    for p in range(HK // 2):
        rows = [pl.multiple_of(idx_smem.at[pl.ds(e * TB, (HK + 1) * TB)][off], ROWS_PER_EXPERT)
                for e in (2 * p, 2 * p + 1)]
        if paired:
            even, odd = (tab[pl.ds(r, 8), :] for r in rows)
            stage[8 * p:8 * (p + 1), :] = jnp.where(sub < ROWS_PER_EXPERT, even, odd)
        else:
            stage[8 * p:8 * p + 4, :] = tab[pl.ds(rows[0], ROWS_PER_EXPERT), :]
            stage[8 * p + 4:8 * (p + 1), :] = tab[pl.ds(rows[1] + ROWS_PER_EXPERT, ROWS_PER_EXPERT), :]


def _start_block(i, n, idx_hbm, tab_hbm, tab, idx_smem, tab_sem, idx_sem):
    slot = lax.rem(i, 2)

    def idx_copy(blk, s):
        return pltpu.make_async_copy(idx_hbm.at[blk], idx_smem.at[pl.ds(s * (HK * TB), HK * TB)], idx_sem.at[s])

    @pl.when(i == 0)
    def _():
        idx_copy(0, 0).start()
        cp = pltpu.make_async_copy(tab_hbm, tab.at[pl.ds(TABLE_PAD, N_EXPERTS * ROWS_PER_EXPERT)], tab_sem)
        cp.start()
        pad = jnp.zeros((TABLE_PAD, LANES), jnp.int32)
        tab[0:TABLE_PAD, :] = pad
        tab[TABLE_PAD + N_EXPERTS * ROWS_PER_EXPERT:, :] = pad
        cp.wait()

    idx_copy(i, slot).wait()

    @pl.when(i + 1 < n)
    def _():
        idx_copy(i + 1, 1 - slot).start()

    return slot * (HK * TB)


def _expert_groups(stage):
    for j in range(HK // 8):
        yield j, [_unpack(stage[pl.ds(8 * ROWS_PER_EXPERT * j + s, 8, stride=ROWS_PER_EXPERT), :])
                  for s in range(ROWS_PER_EXPERT)]


def _peer_u_kernel(eidx_hbm, utab_hbm, h_ref, gates_ref, coef_ref, tab, idx_smem, *scratch):
    N_STAGE = U_STAGES
    stages = scratch[:N_STAGE]
    prods, a_scr, tab_sem, idx_sem = scratch[N_STAGE:]
    base = _start_block(pl.program_id(0), pl.num_programs(0), eidx_hbm, utab_hbm, tab, idx_smem, tab_sem, idx_sem)
    lane = lax.broadcasted_iota(jnp.int32, (HK, TB), 1)
    prods[0] = jnp.zeros((HK, LANES), F32)

    def products(stage, t):
        h = h_ref[t]
        hb = [jnp.broadcast_to(h[r:r + 1, :], (8, LANES)) for r in range(2 * ROWS_PER_EXPERT)]
        for j, rows in _expert_groups(stage):
            acc = None
            for s, (lo, hi) in enumerate(rows):
                term = lo * hb[s] + hi * hb[ROWS_PER_EXPERT + s]
                acc = term if acc is None else acc + term
            prods[t + 1, 8 * j:8 * (j + 1), :] = acc

    def reduce(t):
        col = jnp.sum(prods[t + 1], axis=1, keepdims=True)
        pltpu.store(a_scr, jnp.broadcast_to(col, (HK, TB)), mask=lane == t)

    for t in range(GATHER_AHEAD):
        _gather_rows(tab, idx_smem, stages[t], base + t, paired=U_PAIRED)

    def ring(k, carry):
        for u in range(N_STAGE):
            t = N_STAGE * k + u
            _gather_rows(tab, idx_smem, stages[(u + GATHER_AHEAD) % N_STAGE],
                         base + jnp.minimum(t + GATHER_AHEAD, TB - 1), paired=U_PAIRED)
            products(stages[u], t)
            reduce(t - 1)
        return carry

    lax.fori_loop(0, TB // N_STAGE, ring, 0)
    reduce(TB - 1)
    a = a_scr[...]
    gelu = 0.5 * a * (1.0 + lax.erf(a * (2.0 ** -0.5)))
    coef_ref[0] = gates_ref[0] * gelu


def _peer_v_kernel(eidx_hbm, vtab_hbm, coef_ref, x1_ref, o_ref, tab, idx_smem, *scratch):
    N_STAGE = V_STAGES
    stages = scratch[:N_STAGE]
    cbufs = scratch[N_STAGE:2 * N_STAGE]
    out_tiles, tab_sem, idx_sem = scratch[2 * N_STAGE:]
    base = _start_block(pl.program_id(0), pl.num_programs(0), eidx_hbm, vtab_hbm, tab, idx_smem, tab_sem, idx_sem)
    lane = lax.broadcasted_iota(jnp.int32, (HK, TB), 1)
    out_tiles[...] = x1_ref[...].reshape(TB, D_MODEL // LANES, LANES)

    def fill(stage, cbuf, t):
        _gather_rows(tab, idx_smem, stage, base + t, paired=False)
        col = jnp.sum(jnp.where(lane == t, coef_ref[0], 0.0), axis=1, keepdims=True)
        cbuf[...] = jnp.broadcast_to(col, (HK, LANES))

    def weighted_sum(stage, cbuf, t):
        lo_acc = [None] * ROWS_PER_EXPERT
        hi_acc = [None] * ROWS_PER_EXPERT
        for j, rows in _expert_groups(stage):
            cb = cbuf[8 * j:8 * (j + 1), :]
            for s, (lo, hi) in enumerate(rows):
                lo_acc[s] = lo * cb if lo_acc[s] is None else lo_acc[s] + lo * cb
                hi_acc[s] = hi * cb if hi_acc[s] is None else hi_acc[s] + hi * cb
        rows = [jnp.sum(a, axis=0, keepdims=True) for a in lo_acc + hi_acc]
        out_tiles[t] = out_tiles[t] + jnp.concatenate(rows, axis=0)

    for t in range(GATHER_AHEAD):
        fill(stages[t], cbufs[t], t)

    def ring(k, carry):
        for u in range(N_STAGE):
            t = N_STAGE * k + u
            nxt = (u + GATHER_AHEAD) % N_STAGE
            weighted_sum(stages[u], cbufs[u], t)
            fill(stages[nxt], cbufs[nxt], jnp.minimum(t + GATHER_AHEAD, TB - 1))
        return carry

    lax.fori_loop(0, TB // N_STAGE, ring, 0)
    o_ref[...] = out_tiles[...].reshape(TB, D_MODEL)


def _peer_gather(eidx, gates, h2, x1, utab, vtab):
    nb = TOKENS // TB
    eidx_flat = eidx.reshape(nb, HK * TB)
    any_spec = pl.BlockSpec(memory_space=pl.ANY)
    tok_blk = pl.BlockSpec((TB, 8, LANES), lambda i: (i, 0, 0))
    row_blk = pl.BlockSpec((TB, D_MODEL), lambda i: (i, 0))
    col_blk = pl.BlockSpec((1, HK, TB), lambda i: (i, 0, 0))
    table = pltpu.VMEM((N_EXPERTS * ROWS_PER_EXPERT + 2 * TABLE_PAD, LANES), jnp.int32)
    idx_scr = pltpu.SMEM((2 * HK * TB,), jnp.int32)
    stage = pltpu.VMEM((HK * ROWS_PER_EXPERT, LANES), jnp.int32)
    params = pltpu.CompilerParams(dimension_semantics=("arbitrary",), vmem_limit_bytes=VMEM_LIMIT)
    coef = pl.pallas_call(
        _peer_u_kernel,
        grid=(nb,),
        in_specs=[any_spec, any_spec, tok_blk, col_blk],
        out_specs=col_blk,
        out_shape=jax.ShapeDtypeStruct((nb, HK, TB), F32),
        scratch_shapes=[table, idx_scr] + [stage] * U_STAGES + [
            pltpu.VMEM((TB + 1, HK, LANES), F32), pltpu.VMEM((HK, TB), F32),
            pltpu.SemaphoreType.DMA, pltpu.SemaphoreType.DMA((2,))],
        compiler_params=params,
        name="peer_u",
    )(eidx_flat, utab, h2.reshape(TOKENS, 8, LANES), gates)
    return pl.pallas_call(
        _peer_v_kernel,
        grid=(nb,),
        in_specs=[any_spec, any_spec, col_blk, row_blk],
        out_specs=row_blk,
        out_shape=jax.ShapeDtypeStruct((TOKENS, D_MODEL), F32),
        scratch_shapes=[table, idx_scr] + [stage] * V_STAGES + [pltpu.VMEM((HK, LANES), F32)] * V_STAGES + [
            pltpu.VMEM((TB, D_MODEL // LANES, LANES), F32), pltpu.SemaphoreType.DMA, pltpu.SemaphoreType.DMA((2,))],
        compiler_params=params,
        name="peer_v",
    )(eidx_flat, vtab, coef, x1)


def kernel(x, norm1_g, w_in, q_norm_g, k_norm_g, sgu_norm_g, sgu_w, sgu_b, attn_out_g, sgu_out_g, w_out, norm2_g,
           w_query, sub_keys, expert_u, expert_v):
    b, s, _ = x.shape
    xf = x.reshape(TOKENS, D_MODEL)
    i = jnp.arange(1, ATTN_HEADS + 1, dtype=F32)
    slopes = jnp.exp2(-8.0 * i / ATTN_HEADS)
    g = lax.broadcasted_iota(jnp.int32, (LANES, LANES), 0) // HEAD_DIM
    bd = (g == g.T).astype(BF16)
    for layer in range(norm1_g.shape[0]):
        qn, kn, v, gn = _in_proj(
            xf, norm1_g[layer][None], w_in[layer].astype(BF16),
            jnp.tile(q_norm_g[layer], ATTN_HEADS)[None], jnp.tile(k_norm_g[layer], ATTN_HEADS)[None],
            sgu_norm_g[layer][None], bd, sgu_w[layer],
            jnp.repeat(sgu_b[layer].T, HEAD_DIM, axis=1), sgu_out_g[layer][None])
        attn = _attention(slopes, qn, kn, v)
        x1, h2, qp = _out_proj(attn, gn, xf, attn_out_g[layer][None], w_out[layer].astype(BF16),
                               norm2_g[layer][None], w_query[layer].astype(BF16))
        eidx, gates = _peer_topk(qp, sub_keys[layer].astype(BF16))
        xf = _peer_gather(eidx, gates, h2, x1, _pack_table(expert_u[layer]),
                          _pack_table(expert_v[layer])).reshape(TOKENS, D_MODEL)
    return xf.reshape(b, s, D_MODEL)
```

```python
import jax
import jax.numpy as jnp
from jax import lax
from jax.experimental import pallas as pl
from jax.experimental.pallas import tpu as pltpu

F32 = jnp.float32
BF16 = jnp.bfloat16

D_MODEL = 1024
BATCH = 8
SEQ = 4096
TOKENS = BATCH * SEQ
ATTN_HEADS = 8
HEAD_DIM = 64
ATTN_WIDTH = 512
SGU_GROUPS = 8
SGU_WIDTH = 512
SGU_CHUNK = 128
IN_WIDTH = 3 * ATTN_WIDTH + 2 * SGU_WIDTH
DILATED_BRANCHES = ((128, 1), (512, 4), (2048, 16))
ATTN_BLOCK = 128
ATTN_UNROLL = (32, 32, 16)
PEER_HEADS = 8
N_SUB_KEYS = 128
N_EXPERTS = N_SUB_KEYS * N_SUB_KEYS
PEER_TOPK = 16
HK = PEER_HEADS * PEER_TOPK
U_PAIRED = True
U_STAGES = 2
V_STAGES = 8
GATHER_AHEAD = 1
EPS = 1e-6
NEG_INF = -1e30

LANES = 128
TM = 512
TB = 128
ROWS_PER_EXPERT = 4
TABLE_PAD = 8
PACK_TILE = 512
VMEM_LIMIT = 56 * 1024 * 1024


def _rms(x, gain):
    return x * lax.rsqrt(jnp.mean(x * x, axis=-1, keepdims=True) + EPS) * gain


def _group_rms(t, bd):
    sq = t * t
    hi = sq.astype(BF16)
    lo = (sq - hi.astype(F32)).astype(BF16)
    ssq = jnp.dot(hi, bd, preferred_element_type=F32) + jnp.dot(lo, bd, preferred_element_type=F32)
    return t * lax.rsqrt(ssq * (1.0 / HEAD_DIM) + EPS)


def _in_proj_kernel(x_ref, g1_ref, win_ref, qg_ref, kg_ref, sg_ref, bd_ref, wsgu_ref, bsgu_ref, sog_ref,
                    qn_ref, kn_ref, v_ref, gn_ref):
    h = _rms(x_ref[...], g1_ref[...])
    proj = jnp.dot(h.astype(BF16), win_ref[...], preferred_element_type=F32)
    bd = bd_ref[...]
    for c in range(ATTN_WIDTH // LANES):
        cs = slice(c * LANES, (c + 1) * LANES)
        qn_ref[:, cs] = _group_rms(proj[:, c * LANES:(c + 1) * LANES], bd) * qg_ref[:, cs]
        kn_ref[:, cs] = _group_rms(proj[:, ATTN_WIDTH + c * LANES:ATTN_WIDTH + (c + 1) * LANES], bd) * kg_ref[:, cs]
    v_ref[...] = proj[:, 2 * ATTN_WIDTH:3 * ATTN_WIDTH]

    su0 = 3 * ATTN_WIDTH
    sv0 = su0 + SGU_WIDTH
    row = lax.broadcasted_iota(jnp.int32, (SGU_CHUNK, SGU_CHUNK), 0)
    col = lax.broadcasted_iota(jnp.int32, (SGU_CHUNK, SGU_CHUNK), 1)
    causal = col <= row
    left = lax.broadcasted_iota(jnp.int32, (SGU_CHUNK, LANES), 1) < HEAD_DIM
    gated_cols = []
    for p in range(SGU_WIDTH // LANES):
        cs = slice(p * LANES, (p + 1) * LANES)
        svn = (_group_rms(proj[:, sv0 + p * LANES:sv0 + (p + 1) * LANES], bd) * sg_ref[:, cs]).astype(BF16)
        w0 = jnp.where(causal, wsgu_ref[2 * p], 0.0).astype(BF16)
        w1 = jnp.where(causal, wsgu_ref[2 * p + 1], 0.0).astype(BF16)
        chunks = []
        for c in range(TM // SGU_CHUNK):
            blk = svn[c * SGU_CHUNK:(c + 1) * SGU_CHUNK, :]
            r0 = jnp.dot(w0, blk, preferred_element_type=F32)
            r1 = jnp.dot(w1, blk, preferred_element_type=F32)
            chunks.append(jnp.where(left, r0, r1) + bsgu_ref[:, cs])
        mixed = jnp.concatenate(chunks, axis=0)
        gated_cols.append(proj[:, su0 + p * LANES:su0 + (p + 1) * LANES] * mixed)
    gated = jnp.concatenate(gated_cols, axis=1)
    gn_ref[...] = _rms(gated, sog_ref[...]).astype(BF16)


def _in_proj(x, g1, win, qg, kg, sg, bd, wsgu, bsgu, sog):
    full = lambda *shape: pl.BlockSpec(shape, lambda i: (0,) * len(shape))
    tile = lambda width: pl.BlockSpec((TM, width), lambda i: (i, 0))
    out = jax.ShapeDtypeStruct((TOKENS, ATTN_WIDTH), F32)
    return pl.pallas_call(
        _in_proj_kernel,
        grid=(TOKENS // TM,),
        in_specs=[tile(D_MODEL), full(1, D_MODEL), full(D_MODEL, IN_WIDTH), full(1, ATTN_WIDTH), full(1, ATTN_WIDTH),
                  full(1, SGU_WIDTH), full(LANES, LANES), full(SGU_GROUPS, SGU_CHUNK, SGU_CHUNK),
                  full(SGU_CHUNK, SGU_WIDTH), full(1, SGU_WIDTH)],
        out_specs=[tile(ATTN_WIDTH)] * 4,
        out_shape=[out] * 3 + [jax.ShapeDtypeStruct((TOKENS, SGU_WIDTH), BF16)],
        compiler_params=pltpu.CompilerParams(dimension_semantics=("parallel",), vmem_limit_bytes=VMEM_LIMIT),
        name="in_proj",
    )(x, g1, win, qg, kg, sg, bd, wsgu, bsgu, sog)


def _attn_kernel(slopes_ref, q_ref, k_ref, v_ref, o_ref, *scr):
    hp = pl.program_id(1)
    bias_scr = scr[-1]
    left = lax.broadcasted_iota(jnp.int32, (ATTN_BLOCK, LANES), 1) < HEAD_DIM
    left_kv = lax.broadcasted_iota(jnp.int32, (2 * ATTN_BLOCK, LANES), 1) < HEAD_DIM
    qi = lax.broadcasted_iota(jnp.int32, (ATTN_BLOCK, 2 * ATTN_BLOCK), 0)
    ki = lax.broadcasted_iota(jnp.int32, (ATTN_BLOCK, 2 * ATTN_BLOCK), 1)
    base = ATTN_BLOCK + qi - ki

    for br, (window, d) in enumerate(DILATED_BRANCHES):
        nblk = SEQ // d // ATTN_BLOCK
        w_sub = window // d
        o_scr, m_scr, mx_scr, lx_scr = scr[4 * br:4 * br + 4]

        for later in range(2):
            dist = base - ATTN_BLOCK * (1 - later)
            valid = (dist >= 0) & (dist <= w_sub)
            distf = (d * dist).astype(F32)
            for hh in range(2):
                bias_scr[hh, later] = jnp.where(valid, -slopes_ref[2 * hp + hh] * distf, NEG_INF)

        def rows(start, size):
            return pl.ds(start, size) if d == 1 else pl.ds(start, size, stride=d)

        def block(idx):
            r = idx // nblk
            n = idx % nblk
            later = jnp.minimum(n, 1)
            qs = d * ATTN_BLOCK * n + r
            ks = d * ATTN_BLOCK * (n - later) + r
            qb = q_ref[0, rows(qs, ATTN_BLOCK), :] * (HEAD_DIM ** -0.5)
            kb = k_ref[0, rows(ks, 2 * ATTN_BLOCK), :].astype(BF16)
            vb = v_ref[0, rows(ks, 2 * ATTN_BLOCK), :]
            res = []
            for hh in range(2):
                mine = left if hh == 0 else jnp.logical_not(left)
                mine_kv = left_kv if hh == 0 else jnp.logical_not(left_kv)
                qh = jnp.where(mine, qb, 0.0).astype(BF16)
                s = lax.dot_general(qh, kb, (((1,), (1,)), ((), ())), preferred_element_type=F32)
                s = s + bias_scr[hh, later]
                m = jnp.max(jnp.maximum(s[:, :ATTN_BLOCK], s[:, ATTN_BLOCK:]), axis=-1, keepdims=True)
                p = jnp.exp(s - m).astype(BF16)
                ol = jnp.dot(p, jnp.where(mine_kv, vb, 1.0).astype(BF16), preferred_element_type=F32)
                res.append((ol, m))
            dst = rows(qs, ATTN_BLOCK)
            o_scr[dst, :] = jnp.where(left, res[0][0], res[1][0])
            lx_scr[dst, :] = jnp.where(left, res[1][0], res[0][0])
            m_scr[dst, :] = jnp.where(left, res[0][1], res[1][1])
            mx_scr[dst, :] = jnp.where(left, res[1][1], res[0][1])

        unroll = ATTN_UNROLL[br]

        def body(g, carry):
            for u in range(unroll):
                block(g * unroll + u)
            return carry

        lax.fori_loop(0, SEQ // ATTN_BLOCK // unroll, body, 0)

    o_b, m_b, mx_b, lx_b = [[scr[4 * b + i][...] for b in range(3)] for i in range(4)]
    m_max = jnp.maximum(jnp.maximum(m_b[0], m_b[1]), m_b[2])
    num = sum(jnp.exp(m - m_max) * o for m, o in zip(m_b, o_b))
    mx_max = jnp.maximum(jnp.maximum(mx_b[0], mx_b[1]), mx_b[2])
    den_x = sum(jnp.exp(m - mx_max) * l for m, l in zip(mx_b, lx_b))
    o_ref[0] = num / pltpu.roll(den_x, HEAD_DIM, axis=1)


def _attention(slopes, qn, kn, v):
    blk = pl.BlockSpec((1, SEQ, LANES), lambda b, hp: (b, 0, hp))
    shape3 = (BATCH, SEQ, ATTN_WIDTH)
    return pl.pallas_call(
        _attn_kernel,
        grid=(BATCH, ATTN_WIDTH // LANES),
        in_specs=[pl.BlockSpec(memory_space=pltpu.SMEM), blk, blk, blk],
        out_specs=blk,
        out_shape=jax.ShapeDtypeStruct(shape3, F32),
        scratch_shapes=[pltpu.VMEM((SEQ, LANES), F32)] * 12 + [
            pltpu.VMEM((2, 2, ATTN_BLOCK, 2 * ATTN_BLOCK), F32)],
        compiler_params=pltpu.CompilerParams(dimension_semantics=("parallel", "parallel"),
                                             vmem_limit_bytes=VMEM_LIMIT),
        name="attention",
    )(slopes, qn.reshape(shape3), kn.reshape(shape3), v.reshape(shape3)).reshape(TOKENS, ATTN_WIDTH)


def _out_proj_kernel(attn_ref, gn_ref, x_ref, ag_ref, wout_ref, g2_ref, wq_ref, x1_ref, h2_ref, qp_ref):
    an = _rms(attn_ref[...], ag_ref[...])
    mixed = (jnp.dot(an.astype(BF16), wout_ref[:ATTN_WIDTH, :], preferred_element_type=F32)
             + jnp.dot(gn_ref[...].astype(BF16), wout_ref[ATTN_WIDTH:, :], preferred_element_type=F32))
    x1 = x_ref[...] + mixed
    h2 = _rms(x1, g2_ref[...])
    x1_ref[...] = x1
    h2_ref[...] = h2.reshape(TM, D_MODEL // LANES, LANES)
    qp_ref[...] = jnp.dot(h2.astype(BF16), wq_ref[...], preferred_element_type=F32).astype(BF16)


def _out_proj(attn, gn, x, ag, wout, g2, wq):
    full = lambda *shape: pl.BlockSpec(shape, lambda i: (0,) * len(shape))
    tile = lambda width: pl.BlockSpec((TM, width), lambda i: (i, 0))
    qw = wq.shape[1]
    rows = pl.BlockSpec((TM, D_MODEL // LANES, LANES), lambda i: (i, 0, 0))
    rows_shape = jax.ShapeDtypeStruct((TOKENS, D_MODEL // LANES, LANES), F32)
    return pl.pallas_call(
        _out_proj_kernel,
        grid=(TOKENS // TM,),
        in_specs=[tile(ATTN_WIDTH), tile(SGU_WIDTH), tile(D_MODEL), full(1, ATTN_WIDTH), full(D_MODEL, D_MODEL),
                  full(1, D_MODEL), full(D_MODEL, qw)],
        out_specs=[tile(D_MODEL), rows, tile(qw)],
        out_shape=[jax.ShapeDtypeStruct((TOKENS, D_MODEL), F32), rows_shape,
                   jax.ShapeDtypeStruct((TOKENS, qw), BF16)],
        compiler_params=pltpu.CompilerParams(dimension_semantics=("parallel",), vmem_limit_bytes=VMEM_LIMIT),
        name="out_proj",
    )(attn, gn, x, ag, wout, g2, wq)


def _sort_network(n):
    pairs, p = [], 1
    while p < n:
        k = p
        while k >= 1:
            for j in range(k % p, n - k, 2 * k):
                for i in range(min(k, n - j - k)):
                    if (i + j) // (2 * p) == (i + j + k) // (2 * p):
                        pairs.append((i + j, i + j + k))
            k //= 2
        p *= 2
    return pairs


def _sublane_allreduce(x, op):
    for shift in (4, 2, 1):
        x = op(x, pltpu.roll(x, shift, 0))
    return x


def _top16(vals, ids, n_rows):
    levels = n_rows // 8
    sub = lax.broadcasted_iota(jnp.int32, (8, TB), 0).astype(F32)
    v = [vals[8 * i:8 * (i + 1)] for i in range(levels)]
    p = [sub + 8.0 * i for i in range(levels)]
    payload = [v, p] if ids is None else [v, p, [ids[8 * i:8 * (i + 1)] for i in range(levels)]]
    for i, j in _sort_network(16):
        if j < levels:
            swap = (v[j] > v[i]) | ((v[j] == v[i]) & (p[j] < p[i]))
            for a in payload:
                a[i], a[j] = jnp.where(swap, a[j], a[i]), jnp.where(swap, a[i], a[j])
    best, picked = [], []
    for k in range(PEER_TOPK):
        m = _sublane_allreduce(v[0], jnp.maximum)
        row = _sublane_allreduce(jnp.where(v[0] == m, p[0], float(n_rows)), jnp.minimum)
        hit = p[0] == row
        best.append(m[0:1])
        picked.append(row[0:1] if ids is None else
                      _sublane_allreduce(jnp.where(hit, payload[2][0], -1.0), jnp.maximum)[0:1])
        last = min(levels, PEER_TOPK - k) - 1
        for a in payload:
            for i in range(last):
                a[i] = jnp.where(hit, a[i + 1], a[i])
        v[last] = jnp.where(hit, -jnp.inf, v[last])
    return jnp.concatenate(best, axis=0), jnp.concatenate(picked, axis=0)


def _candidates(v1, i1, v2, i2):
    sub = lax.broadcasted_iota(jnp.int32, (8, TB), 0)
    vals = [v1[0:1] + v2]
    ids = [i1[0:1] * N_SUB_KEYS + i2]
    for a in range(1, 8):
        live = PEER_TOPK // (a + 1)
        c = v1[a:a + 1] + v2[0:8]
        vals.append(c if live >= 8 else jnp.where(sub < live, c, -jnp.inf))
        ids.append(i1[a:a + 1] * N_SUB_KEYS + i2[0:8])
    vals.append(v1[8:16] + v2[0:1])
    ids.append(i1[8:16] * N_SUB_KEYS + i2[0:1])
    return jnp.concatenate(vals, axis=0), jnp.concatenate(ids, axis=0)


def _retrieve_head(h, qp_ref, sk_ref):
    tops = []
    for half in range(2):
        c0 = pl.multiple_of((2 * h + half) * N_SUB_KEYS, N_SUB_KEYS)
        qh = qp_ref[:, pl.ds(c0, N_SUB_KEYS)]
        s = lax.dot_general(sk_ref[half], qh, (((1,), (1,)), ((), ())), preferred_element_type=F32)
        tops.append(_top16(s, None, N_SUB_KEYS))
    (v1, i1), (v2, i2) = tops
    cand, cidx = _candidates(v1, i1, v2, i2)
    best, eid = _top16(cand, cidx, cand.shape[0])
    ex = jnp.exp(best - best[0:1, :])
    odd_slot = lax.broadcasted_iota(jnp.int32, (PEER_TOPK, TB), 0) % 2
    row = eid.astype(jnp.int32) * ROWS_PER_EXPERT + TABLE_PAD - ROWS_PER_EXPERT * odd_slot
    return row, ex / jnp.sum(ex, axis=0, keepdims=True)


def _topk_kernel(qp_ref, sk_ref, eidx_ref, gates_ref):
    for h in range(PEER_HEADS):
        rs = slice(h * PEER_TOPK, (h + 1) * PEER_TOPK)
        eidx_ref[0, rs, :], gates_ref[0, rs, :] = _retrieve_head(h, qp_ref, sk_ref)


def _peer_topk(qp, sk):
    nb = TOKENS // TB
    qw = qp.shape[1]
    out_blk = pl.BlockSpec((1, HK, TB), lambda i: (i, 0, 0))
    return pl.pallas_call(
        _topk_kernel,
        grid=(nb,),
        in_specs=[pl.BlockSpec((TB, qw), lambda i: (i, 0)),
                  pl.BlockSpec((2, N_SUB_KEYS, N_SUB_KEYS), lambda i: (0, 0, 0))],
        out_specs=[out_blk, out_blk],
        out_shape=[jax.ShapeDtypeStruct((nb, HK, TB), jnp.int32), jax.ShapeDtypeStruct((nb, HK, TB), F32)],
        compiler_params=pltpu.CompilerParams(dimension_semantics=("parallel",), vmem_limit_bytes=VMEM_LIMIT),
        name="peer_topk",
    )(qp, sk)


def _pack_kernel(w_ref, o_ref):
    half = D_MODEL // 2
    words = pltpu.pack_elementwise([w_ref[:, :half], w_ref[:, half:]], packed_dtype=BF16)
    for s in range(ROWS_PER_EXPERT):
        o_ref[pl.ds(s, PACK_TILE, stride=ROWS_PER_EXPERT), :] = words[:, s * LANES:(s + 1) * LANES]


def _pack_table(w):
    return pl.pallas_call(
        _pack_kernel,
        grid=(N_EXPERTS // PACK_TILE,),
        in_specs=[pl.BlockSpec((PACK_TILE, D_MODEL), lambda i: (i, 0))],
        out_specs=pl.BlockSpec((PACK_TILE * ROWS_PER_EXPERT, LANES), lambda i: (i, 0)),
        out_shape=jax.ShapeDtypeStruct((N_EXPERTS * ROWS_PER_EXPERT, LANES), jnp.int32),
        compiler_params=pltpu.CompilerParams(dimension_semantics=("parallel",), vmem_limit_bytes=VMEM_LIMIT),
        name="pack_table",
    )(w)


def _unpack(w):
    return (pltpu.unpack_elementwise(w, index=0, packed_dtype=BF16, unpacked_dtype=F32),
            pltpu.unpack_elementwise(w, index=1, packed_dtype=BF16, unpacked_dtype=F32))


def _gather_rows(tab, idx_smem, stage, off, paired):
    sub = lax.broadcasted_iota(jnp.int32, (8, LANES), 0)
    for p in range(HK // 2):
        rows = [pl.multiple_of(idx_smem.at[pl.ds(e * TB, (HK + 1) * TB)][off], ROWS_PER_EXPERT)
                for e in (2 * p, 2 * p + 1)]
        if paired:
            even, odd = (tab[pl.ds(r, 8), :] for r in rows)
            stage[8 * p:8 * (p + 1), :] = jnp.where(sub < ROWS_PER_EXPERT, even, odd)
        else:
            stage[8 * p:8 * p + 4, :] = tab[pl.ds(rows[0], ROWS_PER_EXPERT), :]
            stage[8 * p + 4:8 * (p + 1), :] = tab[pl.ds(rows[1] + ROWS_PER_EXPERT, ROWS_PER_EXPERT), :]


def _start_block(i, n, idx_hbm, tab_hbm, tab, idx_smem, tab_sem, idx_sem):
    slot = lax.rem(i, 2)

    def idx_copy(blk, s):
        return pltpu.make_async_copy(idx_hbm.at[blk], idx_smem.at[pl.ds(s * (HK * TB), HK * TB)], idx_sem.at[s])

    @pl.when(i == 0)
    def _():
        idx_copy(0, 0).start()
        cp = pltpu.make_async_copy(tab_hbm, tab.at[pl.ds(TABLE_PAD, N_EXPERTS * ROWS_PER_EXPERT)], tab_sem)
        cp.start()
        pad = jnp.zeros((TABLE_PAD, LANES), jnp.int32)
        tab[0:TABLE_PAD, :] = pad
        tab[TABLE_PAD + N_EXPERTS * ROWS_PER_EXPERT:, :] = pad
        cp.wait()

    idx_copy(i, slot).wait()

    @pl.when(i + 1 < n)
    def _():
        idx_copy(i + 1, 1 - slot).start()

    return slot * (HK * TB)


def _expert_groups(stage):
    for j in range(HK // 8):
        yield j, [_unpack(stage[pl.ds(8 * ROWS_PER_EXPERT * j + s, 8, stride=ROWS_PER_EXPERT), :])
                  for s in range(ROWS_PER_EXPERT)]


def _peer_u_kernel(eidx_hbm, utab_hbm, h_ref, gates_ref, coef_ref, tab, idx_smem, *scratch):
    N_STAGE = U_STAGES
    stages = scratch[:N_STAGE]
    prods, a_scr, tab_sem, idx_sem = scratch[N_STAGE:]
    base = _start_block(pl.program_id(0), pl.num_programs(0), eidx_hbm, utab_hbm, tab, idx_smem, tab_sem, idx_sem)
    lane = lax.broadcasted_iota(jnp.int32, (HK, TB), 1)
    prods[0] = jnp.zeros((HK, LANES), F32)

    def products(stage, t):
        h = h_ref[t]
        hb = [jnp.broadcast_to(h[r:r + 1, :], (8, LANES)) for r in range(2 * ROWS_PER_EXPERT)]
        for j, rows in _expert_groups(stage):
            acc = None
            for s, (lo, hi) in enumerate(rows):
                term = lo * hb[s] + hi * hb[ROWS_PER_EXPERT + s]
                acc = term if acc is None else acc + term
            prods[t + 1, 8 * j:8 * (j + 1), :] = acc

    def reduce(t):
        col = jnp.sum(prods[t + 1], axis=1, keepdims=True)
        pltpu.store(a_scr, jnp.broadcast_to(col, (HK, TB)), mask=lane == t)

    for t in range(GATHER_AHEAD):
        _gather_rows(tab, idx_smem, stages[t], base + t, paired=U_PAIRED)

    def ring(k, carry):
        for u in range(N_STAGE):
            t = N_STAGE * k + u
            _gather_rows(tab, idx_smem, stages[(u + GATHER_AHEAD) % N_STAGE],
                         base + jnp.minimum(t + GATHER_AHEAD, TB - 1), paired=U_PAIRED)
            products(stages[u], t)
            reduce(t - 1)
        return carry

    lax.fori_loop(0, TB // N_STAGE, ring, 0)
    reduce(TB - 1)
    a = a_scr[...]
    gelu = 0.5 * a * (1.0 + lax.erf(a * (2.0 ** -0.5)))
    coef_ref[0] = gates_ref[0] * gelu


def _peer_v_kernel(eidx_hbm, vtab_hbm, coef_ref, x1_ref, o_ref, tab, idx_smem, *scratch):
    N_STAGE = V_STAGES
    stages = scratch[:N_STAGE]
    cbufs = scratch[N_STAGE:2 * N_STAGE]
    out_tiles, tab_sem, idx_sem = scratch[2 * N_STAGE:]
    base = _start_block(pl.program_id(0), pl.num_programs(0), eidx_hbm, vtab_hbm, tab, idx_smem, tab_sem, idx_sem)
    lane = lax.broadcasted_iota(jnp.int32, (HK, TB), 1)
    out_tiles[...] = x1_ref[...].reshape(TB, D_MODEL // LANES, LANES)

    def fill(stage, cbuf, t):
        _gather_rows(tab, idx_smem, stage, base + t, paired=False)
        col = jnp.sum(jnp.where(lane == t, coef_ref[0], 0.0), axis=1, keepdims=True)
        cbuf[...] = jnp.broadcast_to(col, (HK, LANES))

    def weighted_sum(stage, cbuf, t):
        lo_acc = [None] * ROWS_PER_EXPERT
        hi_acc = [None] * ROWS_PER_EXPERT
        for j, rows in _expert_groups(stage):
            cb = cbuf[8 * j:8 * (j + 1), :]
            for s, (lo, hi) in enumerate(rows):
                lo_acc[s] = lo * cb if lo_acc[s] is None else lo_acc[s] + lo * cb
                hi_acc[s] = hi * cb if hi_acc[s] is None else hi_acc[s] + hi * cb
        rows = [jnp.sum(a, axis=0, keepdims=True) for a in lo_acc + hi_acc]
        out_tiles[t] = out_tiles[t] + jnp.concatenate(rows, axis=0)

    for t in range(GATHER_AHEAD):
        fill(stages[t], cbufs[t], t)

    def ring(k, carry):
        for u in range(N_STAGE):
            t = N_STAGE * k + u
            nxt = (u + GATHER_AHEAD) % N_STAGE
            weighted_sum(stages[u], cbufs[u], t)
            fill(stages[nxt], cbufs[nxt], jnp.minimum(t + GATHER_AHEAD, TB - 1))
        return carry

    lax.fori_loop(0, TB // N_STAGE, ring, 0)
    o_ref[...] = out_tiles[...].reshape(TB, D_MODEL)


def _peer_gather(eidx, gates, h2, x1, utab, vtab):
    nb = TOKENS // TB
    eidx_flat = eidx.reshape(nb, HK * TB)
    any_spec = pl.BlockSpec(memory_space=pl.ANY)
    tok_blk = pl.BlockSpec((TB, 8, LANES), lambda i: (i, 0, 0))
    row_blk = pl.BlockSpec((TB, D_MODEL), lambda i: (i, 0))
    col_blk = pl.BlockSpec((1, HK, TB), lambda i: (i, 0, 0))
    table = pltpu.VMEM((N_EXPERTS * ROWS_PER_EXPERT + 2 * TABLE_PAD, LANES), jnp.int32)
    idx_scr = pltpu.SMEM((2 * HK * TB,), jnp.int32)
    stage = pltpu.VMEM((HK * ROWS_PER_EXPERT, LANES), jnp.int32)
    params = pltpu.CompilerParams(dimension_semantics=("arbitrary",), vmem_limit_bytes=VMEM_LIMIT)
    coef = pl.pallas_call(
        _peer_u_kernel,
        grid=(nb,),
        in_specs=[any_spec, any_spec, tok_blk, col_blk],
        out_specs=col_blk,
        out_shape=jax.ShapeDtypeStruct((nb, HK, TB), F32),
        scratch_shapes=[table, idx_scr] + [stage] * U_STAGES + [
            pltpu.VMEM((TB + 1, HK, LANES), F32), pltpu.VMEM((HK, TB), F32),
            pltpu.SemaphoreType.DMA, pltpu.SemaphoreType.DMA((2,))],
        compiler_params=params,
        name="peer_u",
    )(eidx_flat, utab, h2.reshape(TOKENS, 8, LANES), gates)
    return pl.pallas_call(
        _peer_v_kernel,
        grid=(nb,),
        in_specs=[any_spec, any_spec, col_blk, row_blk],
        out_specs=row_blk,
        out_shape=jax.ShapeDtypeStruct((TOKENS, D_MODEL), F32),
        scratch_shapes=[table, idx_scr] + [stage] * V_STAGES + [pltpu.VMEM((HK, LANES), F32)] * V_STAGES + [
            pltpu.VMEM((TB, D_MODEL // LANES, LANES), F32), pltpu.SemaphoreType.DMA, pltpu.SemaphoreType.DMA((2,))],
        compiler_params=params,
        name="peer_v",
    )(eidx_flat, vtab, coef, x1)


def kernel(x, norm1_g, w_in, q_norm_g, k_norm_g, sgu_norm_g, sgu_w, sgu_b, attn_out_g, sgu_out_g, w_out, norm2_g,
           w_query, sub_keys, expert_u, expert_v):
    b, s, _ = x.shape
    xf = x.reshape(TOKENS, D_MODEL)
    i = jnp.arange(1, ATTN_HEADS + 1, dtype=F32)
    slopes = jnp.exp2(-8.0 * i / ATTN_HEADS)
    g = lax.broadcasted_iota(jnp.int32, (LANES, LANES), 0) // HEAD_DIM
    bd = (g == g.T).astype(BF16)
    for layer in range(norm1_g.shape[0]):
        qn, kn, v, gn = _in_proj(
            xf, norm1_g[layer][None], w_in[layer].astype(BF16),
            jnp.tile(q_norm_g[layer], ATTN_HEADS)[None], jnp.tile(k_norm_g[layer], ATTN_HEADS)[None],
            sgu_norm_g[layer][None], bd, sgu_w[layer],
            jnp.repeat(sgu_b[layer].T, HEAD_DIM, axis=1), sgu_out_g[layer][None])
        attn = _attention(slopes, qn, kn, v)
        x1, h2, qp = _out_proj(attn, gn, xf, attn_out_g[layer][None], w_out[layer].astype(BF16),
                               norm2_g[layer][None], w_query[layer].astype(BF16))
        eidx, gates = _peer_topk(qp, sub_keys[layer].astype(BF16))
        xf = _peer_gather(eidx, gates, h2, x1, _pack_table(expert_u[layer]),
                          _pack_table(expert_v[layer])).reshape(TOKENS, D_MODEL)
    return xf.reshape(b, s, D_MODEL)
```
